```python
import jax
import jax.numpy as jnp
from jax import lax
import numpy as np

D_MODEL = 2048
BATCH = 2
SEQ = 16384
DEPTH = 1

RMS_EPS = 1e-6
N_MEM = 256
SSM_EXPAND = 2
SSM_D_INNER = SSM_EXPAND * D_MODEL
SSM_HEAD_DIM = 64
SSM_N_HEADS = SSM_D_INNER // SSM_HEAD_DIM
SSM_N_GROUPS = 8
SSM_D_STATE = 128
SSM_BC = SSM_N_GROUPS * SSM_D_STATE
SSM_CONV_WIDTH = 4
SSM_CONV_DIM = SSM_D_INNER + 2 * SSM_BC
SSM_CHUNK = 128
RWKV_D = D_MODEL
RWKV_HEAD_DIM = 64
RWKV_N_HEADS = RWKV_D // RWKV_HEAD_DIM
RWKV_DECAY_LORA = 96
RWKV_A_LORA = 96
RWKV_GN_EPS = 64e-5
RWKV_SPLIT_SIZES = (RWKV_D, RWKV_D, RWKV_D, RWKV_D, RWKV_DECAY_LORA, RWKV_A_LORA)
RWKV_IN = 4 * RWKV_D + RWKV_DECAY_LORA + RWKV_A_LORA
MEM_HEADS = 4
MEM_D = D_MODEL
MEM_HEAD_DIM = MEM_D // MEM_HEADS
N_BRANCH = 3
IN_SPLIT_SIZES = (SSM_D_INNER, SSM_CONV_DIM, SSM_N_HEADS, RWKV_IN, MEM_D, MEM_D, N_BRANCH * D_MODEL)
IN_COLS = 4096 + 6144 + 64 + 8384 + 2048 + 2048 + 6144

kernel_name = "hybrid_ssd_rwkv7_memory_block"


def _split(u, sizes):
    points = [int(p) for p in np.cumsum(sizes)[:-1]]
    return jnp.split(u, points, axis=-1)


def rms_norm(u, g, eps=RMS_EPS):
    uf = u.astype(jnp.float32)
    y = uf * lax.rsqrt(jnp.mean(uf * uf, axis=-1, keepdims=True) + eps)
    return (y * g.astype(jnp.float32)).astype(u.dtype)


def gated_group_rms_norm(y, z, g, n_groups, eps=RMS_EPS):
    b, l, c = y.shape
    t = (y * jax.nn.silu(z.astype(jnp.float32))).reshape(b, l, n_groups, c // n_groups)
    t = t * lax.rsqrt(jnp.mean(t * t, axis=-1, keepdims=True) + eps)
    return t.reshape(b, l, c) * g.astype(jnp.float32)


def causal_depthwise_conv(u, w, b):
    k_width, c = w.shape
    out = lax.conv_general_dilated(
        u, w.reshape(k_width, 1, c).astype(u.dtype), window_strides=(1,),
        padding=[(k_width - 1, 0)], dimension_numbers=('NWC', 'WIO', 'NWC'),
        feature_group_count=c)
    return out + b.astype(u.dtype)


def token_shift(u, mu):
    prev = jnp.pad(u, ((0, 0), (1, 0), (0, 0)))[:, :-1]
    return u + (prev - u) * mu.astype(u.dtype)


def ssd_chunked(xh, dt, a_head, bm, cm):
    bsz, seq, n_heads, p = xh.shape
    g, n = bm.shape[2], bm.shape[3]
    j = n_heads // g
    nc = seq // SSM_CHUNK

    def to_chunks(t):
        t = t.reshape((bsz, nc, SSM_CHUNK) + t.shape[2:])
        return jnp.moveaxis(t, 1, 0)

    xc = to_chunks(xh.reshape(bsz, seq, g, j, p))
    dtc = to_chunks(dt.reshape(bsz, seq, g, j))
    bc = to_chunks(bm)
    cc = to_chunks(cm)
    a_g = a_head.reshape(g, j)
    causal = jnp.tril(jnp.ones((SSM_CHUNK, SSM_CHUNK), dtype=bool))[None, :, :, None, None]

    def step(state, inp):
        x_, dt_, b_, c_ = inp
        acs = jnp.cumsum(dt_ * a_g, axis=1)
        seg = acs[:, :, None] - acs[:, None, :]
        decay = jnp.exp(jnp.where(causal, seg, -jnp.inf))
        cb = jnp.einsum('blgn,bsgn->blsg', c_, b_)
        mix = cb[..., None] * decay * dt_[:, None]
        y_intra = jnp.einsum('blsgj,bsgjp->blgjp', mix, x_)
        y_inter = jnp.einsum('blgn,bgjpn->blgjp', c_, state) * jnp.exp(acs)[..., None]
        last = acs[:, -1]
        x_w = x_ * (jnp.exp(last[:, None] - acs) * dt_)[..., None]
        new_state = state * jnp.exp(last)[..., None, None] + jnp.einsum('bsgn,bsgjp->bgjpn', b_, x_w)
        return new_state, y_intra + y_inter

    state0 = jnp.zeros((bsz, g, j, p, n), jnp.float32)
    _, y = lax.scan(step, state0, (xc, dtc, bc, cc))
    return jnp.moveaxis(y, 0, 1).reshape(bsz, seq, n_heads, p)


def rwkv7_scan(r, w, k, v, kk, kka):
    bsz, _, h, n = r.shape

    def step(s, inp):
        r_, w_, k_, v_, kk_, kka_ = inp
        sa = jnp.einsum('bhvk,bhk->bhv', s, -kk_)
        s = s * w_[:, :, None, :] + sa[..., None] * kka_[:, :, None, :] + v_[..., None] * k_[:, :, None, :]
        return s, jnp.einsum('bhvk,bhk->bhv', s, r_)

    xs = tuple(jnp.moveaxis(t, 1, 0) for t in (r, w, k, v, kk, kka))
    s0 = jnp.zeros((bsz, h, n, n), jnp.float32)
    _, y = lax.scan(step, s0, xs)
    return jnp.moveaxis(y, 0, 1)


def hybrid_layer(h, mem, g_pre, w_in, ssm_conv_w, ssm_conv_b, ssm_dt_bias, ssm_a_log, ssm_d, ssm_norm_g,
                 rwkv_mu, rwkv_w0, rwkv_w_up, rwkv_a0, rwkv_a_up, rwkv_k_k, rwkv_k_a, rwkv_r_k,
                 rwkv_gn_g, rwkv_gn_b, mem_norm_g, mem_w_kv, gate_b, w_br_ssm, w_br_rwkv, w_br_mem,
                 w_out, g_post):
    f32 = jnp.float32
    dtype = h.dtype
    bsz, seq, _ = h.shape

    u = rms_norm(h, g_pre)
    proj = u @ w_in
    z, xbc, dt_raw, rwkv_in, mem_q, mem_g, gate_raw = _split(proj, IN_SPLIT_SIZES)

    xbc = jax.nn.silu(causal_depthwise_conv(xbc, ssm_conv_w, ssm_conv_b))
    xs, bm, cm = _split(xbc, (SSM_D_INNER, SSM_BC, SSM_BC))
    dt = jax.nn.softplus(dt_raw.astype(f32) + ssm_dt_bias.astype(f32))
    a_head = -jnp.exp(ssm_a_log.astype(f32))
    xs_h = xs.astype(f32).reshape(bsz, seq, SSM_N_HEADS, SSM_HEAD_DIM)
    y = ssd_chunked(xs_h, dt, a_head,
                    bm.astype(f32).reshape(bsz, seq, SSM_N_GROUPS, SSM_D_STATE),
                    cm.astype(f32).reshape(bsz, seq, SSM_N_GROUPS, SSM_D_STATE))
    y = y + ssm_d.astype(f32)[:, None] * xs_h
    y = gated_group_rms_norm(y.reshape(bsz, seq, SSM_D_INNER), z, ssm_norm_g, SSM_N_GROUPS)
    y_ssm = y.astype(dtype) @ w_br_ssm

    rw = token_shift(rwkv_in, rwkv_mu)
    r, k, v, g, w_lo, a_lo = _split(rw, RWKV_SPLIT_SIZES)
    w_log = -jax.nn.softplus(-(rwkv_w0 + jnp.tanh(w_lo) @ rwkv_w_up).astype(f32)) - 0.5
    w_dec = jnp.exp(-jnp.exp(w_log))
    a = jax.nn.sigmoid((rwkv_a0 + a_lo @ rwkv_a_up).astype(f32))
    r = r.astype(f32)
    k = k.astype(f32)
    v = v.astype(f32)

    def heads(t):
        return t.reshape(bsz, seq, RWKV_N_HEADS, RWKV_HEAD_DIM)

    kk = heads(k * rwkv_k_k.astype(f32))
    kk = kk / jnp.maximum(jnp.sqrt(jnp.sum(kk * kk, axis=-1, keepdims=True)), 1e-12)
    k = k * (1.0 + (a - 1.0) * rwkv_k_a.astype(f32))
    rh, kh, vh, ah = heads(r), heads(k), heads(v), heads(a)
    o = rwkv7_scan(rh, heads(w_dec), kh, vh, kk, kk * ah)
    mu = jnp.mean(o, axis=-1, keepdims=True)
    var = jnp.mean(jnp.square(o - mu), axis=-1, keepdims=True)
    o = ((o - mu) * lax.rsqrt(var + RWKV_GN_EPS)).reshape(bsz, seq, RWKV_D)
    o = o * rwkv_gn_g.astype(f32) + rwkv_gn_b.astype(f32)
    bonus = jnp.sum(rh * kh * rwkv_r_k.astype(f32), axis=-1, keepdims=True) * vh
    o = (o + bonus.reshape(bsz, seq, RWKV_D)) * jax.nn.silu(g.astype(f32))
    y_rwkv = o.astype(dtype) @ w_br_rwkv

    n_mem = mem.shape[1]
    mem_k, mem_v = jnp.split(rms_norm(mem, mem_norm_g) @ mem_w_kv, 2, axis=-1)
    q = mem_q.reshape(bsz, seq, MEM_HEADS, MEM_HEAD_DIM)
    mem_k = mem_k.reshape(bsz, n_mem, MEM_HEADS, MEM_HEAD_DIM)
    mem_v = mem_v.reshape(bsz, n_mem, MEM_HEADS, MEM_HEAD_DIM)
    s = jnp.einsum('blhd,bmhd->bhlm', q, mem_k).astype(f32) * (MEM_HEAD_DIM ** -0.5)
    p = jax.nn.softmax(s, axis=-1).astype(dtype)
    om = jnp.einsum('bhlm,bmhd->blhd', p, mem_v).reshape(bsz, seq, MEM_D)
    y_mem = (om * jax.nn.silu(mem_g)) @ w_br_mem

    gates = jax.nn.sigmoid((gate_raw + gate_b).astype(f32)).astype(dtype)
    gates = gates.reshape(bsz, seq, N_BRANCH, D_MODEL)
    merged = gates[:, :, 0] * y_ssm + gates[:, :, 1] * y_rwkv + gates[:, :, 2] * y_mem
    return h + rms_norm(merged @ w_out, g_post)


def setup_inputs(seed: int = 0) -> dict:
    key = jax.random.key(seed)
    keys = iter(jax.random.split(key, 40))

    def nrm(shape, scale):
        return jax.random.normal(next(keys), shape, jnp.float32) * scale

    def unif(shape, lo, hi):
        return jax.random.uniform(next(keys), shape, jnp.float32, lo, hi)

    dt0 = jnp.exp(unif((DEPTH, SSM_N_HEADS), float(np.log(1e-3)), float(np.log(1e-1))))
    return {
        'x': nrm((BATCH, SEQ, D_MODEL), 1.0),
        'mem': nrm((BATCH, N_MEM, D_MODEL), 1.0),
        'g_pre': 1.0 + nrm((DEPTH, D_MODEL), 0.02),
        'w_in': nrm((DEPTH, D_MODEL, IN_COLS), D_MODEL ** -0.5),
        'ssm_conv_w': nrm((DEPTH, SSM_CONV_WIDTH, SSM_CONV_DIM), SSM_CONV_WIDTH ** -0.5),
        'ssm_conv_b': nrm((DEPTH, SSM_CONV_DIM), 0.02),
        'ssm_dt_bias': dt0 + jnp.log(-jnp.expm1(-dt0)),
        'ssm_a_log': jnp.log(unif((DEPTH, SSM_N_HEADS), 1.0, 16.0)),
        'ssm_d': 1.0 + nrm((DEPTH, SSM_N_HEADS), 0.02),
        'ssm_norm_g': 1.0 + nrm((DEPTH, SSM_D_INNER), 0.02),
        'rwkv_mu': unif((DEPTH, RWKV_IN), 0.0, 1.0),
        'rwkv_w0': unif((DEPTH, RWKV_D), -6.0, 1.0),
        'rwkv_w_up': nrm((DEPTH, RWKV_DECAY_LORA, RWKV_D), 0.1 * RWKV_DECAY_LORA ** -0.5),
        'rwkv_a0': nrm((DEPTH, RWKV_D), 0.1),
        'rwkv_a_up': nrm((DEPTH, RWKV_A_LORA, RWKV_D), RWKV_A_LORA ** -0.5),
        'rwkv_k_k': 0.85 + nrm((DEPTH, RWKV_D), 0.05),
        'rwkv_k_a': 1.0 + nrm((DEPTH, RWKV_D), 0.05),
        'rwkv_r_k': nrm((DEPTH, RWKV_N_HEADS, RWKV_HEAD_DIM), 0.1),
        'rwkv_gn_g': 1.0 + nrm((DEPTH, RWKV_D), 0.02),
        'rwkv_gn_b': nrm((DEPTH, RWKV_D), 0.02),
        'mem_norm_g': 1.0 + nrm((DEPTH, D_MODEL), 0.02),
        'mem_w_kv': nrm((DEPTH, D_MODEL, 2 * MEM_D), D_MODEL ** -0.5),
        'gate_b': nrm((DEPTH, N_BRANCH * D_MODEL), 0.02),
        'w_br_ssm': nrm((DEPTH, SSM_D_INNER, D_MODEL), SSM_D_INNER ** -0.5),
        'w_br_rwkv': nrm((DEPTH, RWKV_D, D_MODEL), RWKV_D ** -0.5),
        'w_br_mem': nrm((DEPTH, MEM_D, D_MODEL), MEM_D ** -0.5),
        'w_out': nrm((DEPTH, D_MODEL, D_MODEL), D_MODEL ** -0.5),
        'g_post': 1.0 + nrm((DEPTH, D_MODEL), 0.02),
    }


def reference(x, mem, g_pre, w_in, ssm_conv_w, ssm_conv_b, ssm_dt_bias, ssm_a_log, ssm_d, ssm_norm_g,
              rwkv_mu, rwkv_w0, rwkv_w_up, rwkv_a0, rwkv_a_up, rwkv_k_k, rwkv_k_a, rwkv_r_k,
              rwkv_gn_g, rwkv_gn_b, mem_norm_g, mem_w_kv, gate_b, w_br_ssm, w_br_rwkv, w_br_mem,
              w_out, g_post):
    layer_params = (g_pre, w_in, ssm_conv_w, ssm_conv_b, ssm_dt_bias, ssm_a_log, ssm_d, ssm_norm_g,
                    rwkv_mu, rwkv_w0, rwkv_w_up, rwkv_a0, rwkv_a_up, rwkv_k_k, rwkv_k_a, rwkv_r_k,
                    rwkv_gn_g, rwkv_gn_b, mem_norm_g, mem_w_kv, gate_b, w_br_ssm, w_br_rwkv, w_br_mem,
                    w_out, g_post)
    h = x
    for layer in range(DEPTH):
        h = hybrid_layer(h, mem, *[p[layer] for p in layer_params])
    return h
```

```python
import functools

import jax
import jax.numpy as jnp
from jax import lax
from jax.experimental import pallas as pl
from jax.experimental.pallas import tpu as pltpu

F32 = jnp.float32
BF16 = jnp.bfloat16

RMS_EPS = 1e-6
SSM_HEAD_DIM = 64
SSM_N_GROUPS = 8
SSM_D_STATE = 128
SSM_CONV_WIDTH = 4
SSM_CHUNK = 128
RWKV_HEAD_DIM = 64
RWKV_LORA = 96
RWKV_GN_EPS = 64e-5
MEM_HEADS = 4

LANES = 128
SUBLANES = 8
VMEM_LIMIT = 56 * 1024 * 1024


def _params(*sem):
    return pltpu.CompilerParams(dimension_semantics=sem, vmem_limit_bytes=VMEM_LIMIT)


def _silu(x):
    return x * jax.nn.sigmoid(x)


def _softplus(x):
    return jnp.maximum(x, 0.0) + jnp.log(1.0 + jnp.exp(-jnp.abs(x)))


def _split_dot(x, w_bf16):
    hi = x.astype(BF16)
    lo = (x - hi.astype(F32)).astype(BF16)
    return (jnp.dot(hi, w_bf16, preferred_element_type=F32)
            + jnp.dot(lo, w_bf16, preferred_element_type=F32))


def _rmsnorm_kernel(x_ref, g_ref, o_ref):
    x = x_ref[...].astype(F32)
    ms = jnp.mean(x * x, axis=-1, keepdims=True)
    o_ref[...] = (x * lax.rsqrt(ms + RMS_EPS) * g_ref[...]).astype(o_ref.dtype)


def _rmsnorm(x2d, g, out_dtype, tm):
    m, d = x2d.shape
    return pl.pallas_call(
        _rmsnorm_kernel,
        grid=(m // tm,),
        in_specs=[pl.BlockSpec((tm, d), lambda i: (i, 0)),
                  pl.BlockSpec((1, d), lambda i: (0, 0))],
        out_specs=pl.BlockSpec((tm, d), lambda i: (i, 0)),
        out_shape=jax.ShapeDtypeStruct((m, d), out_dtype),
        compiler_params=_params("parallel"),
        name="rmsnorm",
    )(x2d, g.reshape(1, d).astype(F32))


def _mm_kernel(a_ref, w_ref, o_ref):
    o_ref[...] = jnp.dot(a_ref[...], w_ref[...],
                         preferred_element_type=F32).astype(o_ref.dtype)


def _matmul(a, w, out_dtype, tm, tn, name):
    m, k = a.shape
    n = w.shape[1]
    tm = min(tm, m)
    tn = min(tn, n)
    return pl.pallas_call(
        _mm_kernel,
        grid=(n // tn, m // tm),
        in_specs=[pl.BlockSpec((tm, k), lambda j, i: (i, 0)),
                  pl.BlockSpec((k, tn), lambda j, i: (0, j))],
        out_specs=pl.BlockSpec((tm, tn), lambda j, i: (i, j)),
        out_shape=jax.ShapeDtypeStruct((m, n), out_dtype),
        compiler_params=_params("parallel", "parallel"),
        name=name,
    )(a, w)


def _ssd_kernel(xbc_ref, z_ref, sm_ref, cw_ref, cb_ref, dtb_ref, alog_ref, d_ref, ng_ref,
                o_ref, xpad_ref, state_ref, y_ref):
    c = pl.program_id(1)
    d_inner = z_ref.shape[2]
    d_bc = SSM_N_GROUPS * SSM_D_STATE
    n_pairs = d_inner // LANES
    pairs_per_group = n_pairs // SSM_N_GROUPS
    L = SSM_CHUNK

    @pl.when(c == 0)
    def _():
        xpad_ref[0:SUBLANES, :] = jnp.zeros((SUBLANES, xpad_ref.shape[1]), F32)
        state_ref[...] = jnp.zeros(state_ref.shape, F32)

    xpad_ref[SUBLANES:SUBLANES + L, :] = xbc_ref[0]
    acc = cb_ref[...] + cw_ref[3:4, :] * xpad_ref[SUBLANES:SUBLANES + L, :]
    for kk in range(SSM_CONV_WIDTH - 1):
        shift = SSM_CONV_WIDTH - 1 - kk
        acc = acc + cw_ref[kk:kk + 1, :] * xpad_ref[SUBLANES - shift:SUBLANES - shift + L, :]
    xpad_ref[0:SUBLANES, :] = xpad_ref[L:L + SUBLANES, :]
    act = _silu(acc)

    row = lax.broadcasted_iota(jnp.int32, (L, L), 0)
    col = lax.broadcasted_iota(jnp.int32, (L, L), 1)
    causal = col <= row
    tri = causal.astype(F32)
    first_head = col < SSM_HEAD_DIM

    dt = _softplus(sm_ref[0][:, :LANES] + dtb_ref[...])
    a_neg = -jnp.exp(alog_ref[...])
    acs = jnp.dot(tri, dt * a_neg, preferred_element_type=F32,
                  precision=lax.Precision.HIGHEST)
    acs_t = acs.T
    dt_t = dt.T

    for g in range(SSM_N_GROUPS):
        bg = act[:, d_inner + g * SSM_D_STATE:d_inner + (g + 1) * SSM_D_STATE]
        cg = act[:, d_inner + d_bc + g * SSM_D_STATE:d_inner + d_bc + (g + 1) * SSM_D_STATE]
        bg_b = bg.astype(BF16)
        cg_b = cg.astype(BF16)
        cb = lax.dot_general(cg_b, bg_b, (((1,), (1,)), ((), ())),
                             preferred_element_type=F32)
        bgt_b = bg.T.astype(BF16)
        for q in range(pairs_per_group):
            p = g * pairs_per_group + q
            xp = act[:, p * LANES:(p + 1) * LANES]
            xp_b = xp.astype(BF16)
            ys = []
            cols = []
            dcols = []
            for e in range(2):
                h = 2 * p + e
                col_b = jnp.broadcast_to(acs[:, h:h + 1], (L, L))
                row_b = jnp.broadcast_to(acs_t[h:h + 1, :], (L, L))
                dtrow_b = jnp.broadcast_to(dt_t[h:h + 1, :], (L, L))
                decay = jnp.where(causal, jnp.exp(col_b - row_b), 0.0)
                mix = (cb * decay * dtrow_b).astype(BF16)
                ys.append(jnp.dot(mix, xp_b, preferred_element_type=F32))
                cols.append(col_b)
                dcols.append(jnp.broadcast_to(dt[:, h:h + 1], (L, L)))
            y_intra = jnp.where(first_head, ys[0], ys[1])
            colsel = jnp.where(first_head, cols[0], cols[1])
            dtsel = jnp.where(first_head, dcols[0], dcols[1])
            st = state_ref[p]
            y_inter = jnp.dot(cg_b, st.astype(BF16), preferred_element_type=F32) * jnp.exp(colsel)
            y_ref[:, p * LANES:(p + 1) * LANES] = (
                y_intra + y_inter + d_ref[:, p * LANES:(p + 1) * LANES] * xp)
            last = colsel[L - 1:L, :]
            xw = (xp * (jnp.exp(last - colsel) * dtsel)).astype(BF16)
            state_ref[p] = st * jnp.exp(last) + jnp.dot(bgt_b, xw, preferred_element_type=F32)

    gw = d_inner // SSM_N_GROUPS
    for g in range(SSM_N_GROUPS):
        sl = slice(g * gw, (g + 1) * gw)
        t = y_ref[:, sl] * _silu(z_ref[0][:, sl])
        ms = jnp.mean(t * t, axis=-1, keepdims=True)
        o_ref[0, :, sl] = (t * lax.rsqrt(ms + RMS_EPS) * ng_ref[:, sl]).astype(o_ref.dtype)


def _ssd_branch(xbc, z, small, conv_w, conv_b, dt_bias, a_log, d_skip, norm_g):
    bsz, seq, conv_dim = xbc.shape
    d_inner = z.shape[2]
    n_heads = d_inner // SSM_HEAD_DIM
    nc = seq // SSM_CHUNK
    pad = LANES - n_heads
    dtb = jnp.pad(dt_bias.astype(F32), (0, pad)).reshape(1, LANES)
    alog = jnp.pad(a_log.astype(F32), (0, pad)).reshape(1, LANES)
    d_row = jnp.repeat(d_skip.astype(F32), SSM_HEAD_DIM).reshape(1, d_inner)
    full = lambda b, c: (0, 0)
    return pl.pallas_call(
        _ssd_kernel,
        grid=(bsz, nc),
        in_specs=[pl.BlockSpec((1, SSM_CHUNK, conv_dim), lambda b, c: (b, c, 0)),
                  pl.BlockSpec((1, SSM_CHUNK, d_inner), lambda b, c: (b, c, 0)),
                  pl.BlockSpec((1, SSM_CHUNK, small.shape[2]), lambda b, c: (b, c, 0)),
                  pl.BlockSpec((SSM_CONV_WIDTH, conv_dim), full),
                  pl.BlockSpec((1, conv_dim), full),
                  pl.BlockSpec((1, LANES), full),
                  pl.BlockSpec((1, LANES), full),
                  pl.BlockSpec((1, d_inner), full),
                  pl.BlockSpec((1, d_inner), full)],
        out_specs=pl.BlockSpec((1, SSM_CHUNK, d_inner), lambda b, c: (b, c, 0)),
        out_shape=jax.ShapeDtypeStruct((bsz, seq, d_inner), BF16),
        scratch_shapes=[pltpu.VMEM((SSM_CHUNK + 2 * SUBLANES, conv_dim), F32),
                        pltpu.VMEM((d_inner // LANES, SSM_D_STATE, LANES), F32),
                        pltpu.VMEM((SSM_CHUNK, d_inner), F32)],
        compiler_params=_params("arbitrary", "arbitrary"),
        name="ssd_scan",
    )(xbc, z, small, conv_w.astype(F32), conv_b.reshape(1, conv_dim).astype(F32),
      dtb, alog, d_row, norm_g.reshape(1, d_inner).astype(F32))


def _head_sum(x, ind_ref, indt_ref):
    s = _split_dot(x, ind_ref[...])
    return _split_dot(s, indt_ref[...])


def _rwkv_prep_kernel(x_ref, xh_ref, s_ref, sh_ref, mu_ref, mus_ref, w0_ref, wup_ref, a0_ref,
                      aup_ref, kk_ref, ka_ref, rk_ref, ind_ref, indt_ref,
                      r_o, w_o, k_o, v_o, nkk_o, kka_o, bonus_o, gs_o):
    i = pl.program_id(1)
    tm = x_ref.shape[1]
    d = r_o.shape[2]
    not_first = (i > 0).astype(F32)

    def shifted(cur, halo):
        prev_row = halo[SUBLANES - 1:SUBLANES, :] * not_first
        rolled = pltpu.roll(cur, 1, 0)
        rid = lax.broadcasted_iota(jnp.int32, cur.shape, 0)
        return jnp.where(rid == 0, prev_row, rolled)

    x = x_ref[0]
    rw = x + (shifted(x, xh_ref[0]) - x) * mu_ref[...]
    s = s_ref[0]
    rs = s + (shifted(s, sh_ref[0]) - s) * mus_ref[...]

    r = rw[:, 0:d]
    k = rw[:, d:2 * d]
    v = rw[:, 2 * d:3 * d]
    g = rw[:, 3 * d:4 * d]

    w_arg = w0_ref[...] + jnp.dot(jnp.tanh(rs), wup_ref[...], preferred_element_type=F32,
                                  precision=lax.Precision.HIGHEST)
    w_log = -_softplus(-w_arg) - 0.5
    w_o[0] = jnp.exp(-jnp.exp(w_log))
    a = jax.nn.sigmoid(a0_ref[...] + jnp.dot(rs, aup_ref[...], preferred_element_type=F32,
                                             precision=lax.Precision.HIGHEST))

    kk = k * kk_ref[...]
    nrm = jnp.sqrt(_head_sum(kk * kk, ind_ref, indt_ref))
    kk = kk / jnp.maximum(nrm, 1e-12)
    k2 = k * (1.0 + (a - 1.0) * ka_ref[...])
    r_o[0] = r
    k_o[0] = k2
    v_o[0] = v
    nkk_o[0] = -kk
    kka_o[0] = kk * a
    bonus_o[0] = _head_sum(r * k2 * rk_ref[...], ind_ref, indt_ref) * v
    gs_o[0] = _silu(g).astype(gs_o.dtype)


def _rwkv_scan_kernel(r_ref, w_ref, k_ref, v_ref, nkk_ref, kka_ref, e_ref, j_ref, m8_ref,
                      y_ref, s_ref):
    bsz, tl, d = r_ref.shape
    uw = 4 * RWKV_HEAD_DIM
    n_units = d // uw

    @pl.when(pl.program_id(0) == 0)
    def _():
        s_ref[...] = jnp.zeros(s_ref.shape, F32)

    e_mask = e_ref[...]
    j_blk = j_ref[...]
    m8 = m8_ref[...]

    def step(t, carry):
        for b in range(bsz):
            for u in range(n_units):
                sl = slice(u * uw, (u + 1) * uw)
                idx = b * n_units + u
                row = lambda ref: ref[b, pl.ds(t, 1), sl]
                s = s_ref[idx]
                p1 = (s * row(nkk_ref)).astype(BF16)
                ev = (e_mask * row(v_ref)).astype(BF16)
                res = jnp.dot(jnp.concatenate([p1, ev], axis=0), j_blk,
                              preferred_element_type=F32)
                sa_b = res[0:RWKV_HEAD_DIM]
                v_b = res[RWKV_HEAD_DIM:2 * RWKV_HEAD_DIM]
                s2 = s * row(w_ref) + sa_b * row(kka_ref) + v_b * row(k_ref)
                s_ref[idx] = s2
                r4 = (m8 * row(r_ref)).astype(BF16)
                o = lax.dot_general(r4, s2.astype(BF16), (((1,), (1,)), ((), ())),
                                    preferred_element_type=F32)
                o4 = jnp.concatenate([o, o, o, o], axis=1) * m8
                y_ref[b, pl.ds(t, 1), sl] = jnp.sum(o4, axis=0, keepdims=True)
        return carry

    lax.fori_loop(0, tl, step, 0)


def _rwkv_post_kernel(y_ref, bonus_ref, gs_ref, gg_ref, gb_ref, ind_ref, indt_ref, o_ref):
    y = y_ref[...]
    inv = 1.0 / RWKV_HEAD_DIM
    mu = _head_sum(y, ind_ref, indt_ref) * inv
    dlt = y - mu
    var = _head_sum(dlt * dlt, ind_ref, indt_ref) * inv
    o = dlt * lax.rsqrt(var + RWKV_GN_EPS) * gg_ref[...] + gb_ref[...]
    o_ref[...] = ((o + bonus_ref[...]) * gs_ref[...].astype(F32)).astype(o_ref.dtype)


def _rwkv_branch(rkvg, small, mu, w0, w_up, a0, a_up, k_k, k_a, r_k, gn_g, gn_b):
    bsz, seq, d4 = rkvg.shape
    d = d4 // 4
    n_heads = d // RWKV_HEAD_DIM
    sw = small.shape[2]
    lo = sw - 2 * RWKV_LORA
    mu = mu.astype(F32)
    mu_main = mu[:4 * d].reshape(1, 4 * d)
    mu_small = jnp.pad(mu[4 * d:], (lo, 0)).reshape(1, sw)
    wup_pad = jnp.pad(w_up.astype(F32), ((lo, RWKV_LORA), (0, 0)))
    aup_pad = jnp.pad(a_up.astype(F32), ((lo + RWKV_LORA, 0), (0, 0)))
    head_of = jnp.arange(d) // RWKV_HEAD_DIM
    ind = (head_of[:, None] == jnp.arange(LANES)[None, :]).astype(BF16)
    indt = ind.T
    vec = lambda t: t.reshape(1, d).astype(F32)

    tm = 128
    nt = seq // tm
    hb = tm // SUBLANES
    cur = lambda w: pl.BlockSpec((1, tm, w), lambda b, i: (b, i, 0))
    halo = lambda w: pl.BlockSpec((1, SUBLANES, w), lambda b, i: (b, jnp.maximum(i * hb - 1, 0), 0))
    full = lambda s0, s1: pl.BlockSpec((s0, s1), lambda b, i: (0, 0))
    outs = pl.pallas_call(
        _rwkv_prep_kernel,
        grid=(bsz, nt),
        in_specs=[cur(4 * d), halo(4 * d), cur(sw), halo(sw), full(1, 4 * d), full(1, sw),
                  full(1, d), full(sw, d), full(1, d), full(sw, d), full(1, d), full(1, d),
                  full(1, d), full(d, LANES), full(LANES, d)],
        out_specs=[cur(d)] * 8,
        out_shape=[jax.ShapeDtypeStruct((bsz, seq, d), F32)] * 7
                  + [jax.ShapeDtypeStruct((bsz, seq, d), BF16)],
        compiler_params=_params("parallel", "parallel"),
        name="rwkv_prep",
    )(rkvg, rkvg, small, small, mu_main, mu_small, vec(w0), wup_pad, vec(a0), aup_pad,
      vec(k_k), vec(k_a), vec(r_k), ind, indt)
    r, w, k, v, nkk, kka, bonus, gs = outs

    uw = 4 * RWKV_HEAD_DIM
    lane = jnp.arange(uw)
    e_mask = (lane[None, :] % RWKV_HEAD_DIM == jnp.arange(RWKV_HEAD_DIM)[:, None]).astype(F32)
    j_blk = (lane[:, None] // RWKV_HEAD_DIM == lane[None, :] // RWKV_HEAD_DIM).astype(BF16)
    m8 = (lane[None, :] // RWKV_HEAD_DIM == jnp.arange(SUBLANES)[:, None]).astype(F32)
    tl = 64
    blk = pl.BlockSpec((bsz, tl, d), lambda i: (0, i, 0))
    cst = lambda a: pl.BlockSpec(a.shape, lambda i: (0, 0))
    y = pl.pallas_call(
        _rwkv_scan_kernel,
        grid=(seq // tl,),
        in_specs=[blk] * 6 + [cst(e_mask), cst(j_blk), cst(m8)],
        out_specs=blk,
        out_shape=jax.ShapeDtypeStruct((bsz, seq, d), F32),
        scratch_shapes=[pltpu.VMEM((bsz * d // uw, RWKV_HEAD_DIM, uw), F32)],
        compiler_params=_params("arbitrary"),
        name="rwkv_scan",
    )(r, w, k, v, nkk, kka, e_mask, j_blk, m8)

    m = bsz * seq
    tp = 256
    rows = lambda: pl.BlockSpec((tp, d), lambda i: (i, 0))
    c1 = lambda s0, s1: pl.BlockSpec((s0, s1), lambda i: (0, 0))
    return pl.pallas_call(
        _rwkv_post_kernel,
        grid=(m // tp,),
        in_specs=[rows(), rows(), rows(), c1(1, d), c1(1, d), c1(d, LANES), c1(LANES, d)],
        out_specs=rows(),
        out_shape=jax.ShapeDtypeStruct((m, d), BF16),
        compiler_params=_params("parallel"),
        name="rwkv_post",
    )(y.reshape(m, d), bonus.reshape(m, d), gs.reshape(m, d), vec(gn_g), vec(gn_b), ind, indt)


def _mem_attn_kernel(q_ref, k_ref, v_ref, g_ref, o_ref):
    d = q_ref.shape[2]
    hd = d // MEM_HEADS
    scale = hd ** -0.5
    for h in range(MEM_HEADS):
        sl = slice(h * hd, (h + 1) * hd)
        s = lax.dot_general(q_ref[0, :, sl], k_ref[0, :, sl], (((1,), (1,)), ((), ())),
                            preferred_element_type=F32) * scale
        s = s - jnp.max(s, axis=-1, keepdims=True)
        e = jnp.exp(s)
        p = e / jnp.sum(e, axis=-1, keepdims=True)
        om = jnp.dot(p.astype(BF16), v_ref[0, :, sl], preferred_element_type=F32)
        o_ref[0, :, sl] = (om * _silu(g_ref[0, :, sl])).astype(o_ref.dtype)


def _mem_branch(q, gate, mem_k, mem_v):
    bsz, seq, d = q.shape
    n_mem = mem_k.shape[1]
    tm = 512
    cur = pl.BlockSpec((1, tm, d), lambda b, i: (b, i, 0))
    kv = pl.BlockSpec((1, n_mem, d), lambda b, i: (b, 0, 0))
    return pl.pallas_call(
        _mem_attn_kernel,
        grid=(bsz, seq // tm),
        in_specs=[cur, kv, kv, cur],
        out_specs=cur,
        out_shape=jax.ShapeDtypeStruct((bsz, seq, d), BF16),
        compiler_params=_params("parallel", "parallel"),
        name="mem_attn",
    )(q, mem_k, mem_v, gate)


def _merge_kernel(a1_ref, a2_ref, a3_ref, w1_ref, w2_ref, w3_ref, g1_ref, g2_ref, g3_ref,
                  b1_ref, b2_ref, b3_ref, o_ref):
    def term(a_ref, w_ref, g_ref, b_ref):
        y = jnp.dot(a_ref[...], w_ref[...], preferred_element_type=F32)
        return jax.nn.sigmoid(g_ref[...] + b_ref[...]) * y

    acc = term(a1_ref, w1_ref, g1_ref, b1_ref)
    acc = acc + term(a2_ref, w2_ref, g2_ref, b2_ref)
    acc = acc + term(a3_ref, w3_ref, g3_ref, b3_ref)
    o_ref[...] = acc.astype(o_ref.dtype)


def _merge(a_ssm, a_rwkv, a_mem, w_ssm, w_rwkv, w_mem, gate_raw, gate_b):
    m, d = a_rwkv.shape
    tm, tn = 512, 512
    nb = d // tn
    act = lambda a: pl.BlockSpec((tm, a.shape[1]), lambda j, i: (i, 0))
    wgt = lambda w: pl.BlockSpec((w.shape[0], tn), lambda j, i: (0, j))
    gate = lambda k: pl.BlockSpec((tm, tn), lambda j, i: (i, j + k * nb))
    bias = lambda k: pl.BlockSpec((1, tn), lambda j, i: (0, j + k * nb))
    gb = gate_b.reshape(1, 3 * d).astype(F32)
    return pl.pallas_call(
        _merge_kernel,
        grid=(nb, m // tm),
        in_specs=[act(a_ssm), act(a_rwkv), act(a_mem), wgt(w_ssm), wgt(w_rwkv), wgt(w_mem),
                  gate(0), gate(1), gate(2), bias(0), bias(1), bias(2)],
        out_specs=pl.BlockSpec((tm, tn), lambda j, i: (i, j)),
        out_shape=jax.ShapeDtypeStruct((m, d), BF16),
        compiler_params=_params("parallel", "parallel"),
        name="merge",
    )(a_ssm, a_rwkv, a_mem, w_ssm, w_rwkv, w_mem, gate_raw, gate_raw, gate_raw, gb, gb, gb)


def _out_kernel(m_ref, w_ref, g_ref, x_ref, o_ref):
    y = jnp.dot(m_ref[...], w_ref[...], preferred_element_type=F32)
    ms = jnp.mean(y * y, axis=-1, keepdims=True)
    o_ref[...] = x_ref[...] + y * lax.rsqrt(ms + RMS_EPS) * g_ref[...]


def _out_proj(merged, w_out, g_post, x2d):
    m, d = x2d.shape
    tm = 512
    rows = pl.BlockSpec((tm, d), lambda i: (i, 0))
    return pl.pallas_call(
        _out_kernel,
        grid=(m // tm,),
        in_specs=[rows, pl.BlockSpec((d, d), lambda i: (0, 0)),
                  pl.BlockSpec((1, d), lambda i: (0, 0)), rows],
        out_specs=rows,
        out_shape=jax.ShapeDtypeStruct((m, d), F32),
        compiler_params=_params("parallel"),
        name="out_proj",
    )(merged, w_out, g_post.reshape(1, d).astype(F32), x2d)


def _layer(h, mem, g_pre, w_in, ssm_conv_w, ssm_conv_b, ssm_dt_bias, ssm_a_log, ssm_d, ssm_norm_g,
           rwkv_mu, rwkv_w0, rwkv_w_up, rwkv_a0, rwkv_a_up, rwkv_k_k, rwkv_k_a, rwkv_r_k,
           rwkv_gn_g, rwkv_gn_b, mem_norm_g, mem_w_kv, gate_b, w_br_ssm, w_br_rwkv, w_br_mem,
           w_out, g_post):
    bsz, seq, d = h.shape
    m = bsz * seq
    d_inner = w_br_ssm.shape[0]
    n_ssm_heads = d_inner // SSM_HEAD_DIM
    conv_dim = ssm_conv_w.shape[1]
    d_rwkv = w_br_rwkv.shape[0]
    d_mem = w_br_mem.shape[0]
    x2d = h.reshape(m, d)

    o_z = 0
    o_xbc = o_z + d_inner
    o_dt = o_xbc + conv_dim
    o_rw = o_dt + n_ssm_heads
    o_lora = o_rw + 4 * d_rwkv
    o_q = o_lora + 2 * RWKV_LORA
    o_mg = o_q + d_mem
    o_gate = o_mg + d_mem
    wb = lambda a, b: w_in[:, a:b].astype(BF16)
    w_small = jnp.concatenate([w_in[:, o_dt:o_rw], w_in[:, o_lora:o_q]], axis=1).astype(BF16)

    u = _rmsnorm(x2d, g_pre, BF16, 512)
    z = _matmul(u, wb(o_z, o_xbc), F32, 1024, 1024, "proj_z")
    xbc = _matmul(u, wb(o_xbc, o_dt), F32, 1024, 1024, "proj_xbc")
    rkvg = _matmul(u, wb(o_rw, o_lora), F32, 1024, 1024, "proj_rwkv")
    small = _matmul(u, w_small, F32, 1024, w_small.shape[1], "proj_small")
    mem_q = _matmul(u, wb(o_q, o_mg), BF16, 1024, 1024, "proj_memq")
    mem_g = _matmul(u, wb(o_mg, o_gate), F32, 1024, 1024, "proj_memg")
    gate_raw = _matmul(u, wb(o_gate, w_in.shape[1]), F32, 1024, 1024, "proj_gate")

    a_ssm = _ssd_branch(xbc.reshape(bsz, seq, conv_dim), z.reshape(bsz, seq, d_inner),
                        small.reshape(bsz, seq, -1), ssm_conv_w, ssm_conv_b, ssm_dt_bias,
                        ssm_a_log, ssm_d, ssm_norm_g).reshape(m, d_inner)

    a_rwkv = _rwkv_branch(rkvg.reshape(bsz, seq, 4 * d_rwkv), small.reshape(bsz, seq, -1),
                          rwkv_mu, rwkv_w0, rwkv_w_up, rwkv_a0, rwkv_a_up, rwkv_k_k, rwkv_k_a,
                          rwkv_r_k, rwkv_gn_g, rwkv_gn_b)

    n_mem = mem.shape[1]
    mem_n = _rmsnorm(mem.reshape(bsz * n_mem, d), mem_norm_g, BF16, n_mem)
    mem_kv = _matmul(mem_n, mem_w_kv.astype(BF16), BF16, bsz * n_mem, 1024, "proj_memkv")
    mem_k = mem_kv[:, :d_mem].reshape(bsz, n_mem, d_mem)
    mem_v = mem_kv[:, d_mem:].reshape(bsz, n_mem, d_mem)
    a_mem = _mem_branch(mem_q.reshape(bsz, seq, d_mem), mem_g.reshape(bsz, seq, d_mem),
                        mem_k, mem_v).reshape(m, d_mem)

    merged = _merge(a_ssm, a_rwkv, a_mem, w_br_ssm.astype(BF16), w_br_rwkv.astype(BF16),
                    w_br_mem.astype(BF16), gate_raw, gate_b)
    out = _out_proj(merged, w_out.astype(BF16), g_post, x2d)
    return out.reshape(bsz, seq, d)


def kernel(x, mem, g_pre, w_in, ssm_conv_w, ssm_conv_b, ssm_dt_bias, ssm_a_log, ssm_d, ssm_norm_g, rwkv_mu, rwkv_w0, rwkv_w_up, rwkv_a0, rwkv_a_up, rwkv_k_k, rwkv_k_a, rwkv_r_k, rwkv_gn_g, rwkv_gn_b, mem_norm_g, mem_w_kv, gate_b, w_br_ssm, w_br_rwkv, w_br_mem, w_out, g_post):
    layer_params = (g_pre, w_in, ssm_conv_w, ssm_conv_b, ssm_dt_bias, ssm_a_log, ssm_d, ssm_norm_g,
                    rwkv_mu, rwkv_w0, rwkv_w_up, rwkv_a0, rwkv_a_up, rwkv_k_k, rwkv_k_a, rwkv_r_k,
                    rwkv_gn_g, rwkv_gn_b, mem_norm_g, mem_w_kv, gate_b, w_br_ssm, w_br_rwkv,
                    w_br_mem, w_out, g_post)
    h = x
    for layer in range(g_pre.shape[0]):
        h = _layer(h, mem, *[p[layer] for p in layer_params])
    return h
```

```python
import functools

import jax
import jax.numpy as jnp
from jax import lax
from jax.experimental import pallas as pl
from jax.experimental.pallas import tpu as pltpu

F32 = jnp.float32
BF16 = jnp.bfloat16

RMS_EPS = 1e-6
SSM_HEAD_DIM = 64
SSM_N_GROUPS = 8
SSM_D_STATE = 128
SSM_CONV_WIDTH = 4
SSM_CHUNK = 128
RWKV_HEAD_DIM = 64
RWKV_LORA = 96
RWKV_GN_EPS = 64e-5
RWKV_CHUNK = 64
UNIT_HEADS = 4
UNIT_W = UNIT_HEADS * RWKV_HEAD_DIM
MEM_HEADS = 4

LANES = 128
SUBLANES = 8
VMEM_LIMIT = 56 * 1024 * 1024
Q_STRIDE = LANES + SUBLANES


def _params(*sem):
    return pltpu.CompilerParams(dimension_semantics=sem, vmem_limit_bytes=VMEM_LIMIT)


def _silu(x):
    return x * jax.nn.sigmoid(x)


def _softplus(x):
    return jnp.maximum(x, 0.0) + jnp.log(1.0 + jnp.exp(-jnp.abs(x)))


def _split_dot(x, w_bf16):
    hi = x.astype(BF16)
    lo = (x - hi.astype(F32)).astype(BF16)
    return (jnp.dot(hi, w_bf16, preferred_element_type=F32)
            + jnp.dot(lo, w_bf16, preferred_element_type=F32))


def _rmsnorm_kernel(x_ref, g_ref, o_ref):
    x = x_ref[...].astype(F32)
    ms = jnp.mean(x * x, axis=-1, keepdims=True)
    o_ref[...] = (x * lax.rsqrt(ms + RMS_EPS) * g_ref[...]).astype(o_ref.dtype)


def _rmsnorm(x2d, g, out_dtype, tm):
    m, d = x2d.shape
    return pl.pallas_call(
        _rmsnorm_kernel,
        grid=(m // tm,),
        in_specs=[pl.BlockSpec((tm, d), lambda i: (i, 0)),
                  pl.BlockSpec((1, d), lambda i: (0, 0))],
        out_specs=pl.BlockSpec((tm, d), lambda i: (i, 0)),
        out_shape=jax.ShapeDtypeStruct((m, d), out_dtype),
        compiler_params=_params("parallel"),
        name="rmsnorm",
    )(x2d, g.reshape(1, d).astype(F32))


def _mm_kernel(a_ref, w_ref, o_ref):
    o_ref[...] = jnp.dot(a_ref[...], w_ref[...],
                         preferred_element_type=F32).astype(o_ref.dtype)


def _matmul(a, w, out_dtype, tm, tn, name):
    m, k = a.shape
    n = w.shape[1]
    tm = min(tm, m)
    tn = min(tn, n)
    return pl.pallas_call(
        _mm_kernel,
        grid=(n // tn, m // tm),
        in_specs=[pl.BlockSpec((tm, k), lambda j, i: (i, 0)),
                  pl.BlockSpec((k, tn), lambda j, i: (0, j))],
        out_specs=pl.BlockSpec((tm, tn), lambda j, i: (i, j)),
        out_shape=jax.ShapeDtypeStruct((m, n), out_dtype),
        compiler_params=_params("parallel", "parallel"),
        name=name,
    )(a, w)


def _ssd_kernel(xbc_ref, z_ref, sm_ref, cw_ref, cb_ref, dtb_ref, alog_ref, d_ref, ng_ref,
                o_ref, xpad_ref, state_ref, y_ref):
    c = pl.program_id(1)
    d_inner = z_ref.shape[2]
    d_bc = SSM_N_GROUPS * SSM_D_STATE
    n_pairs = d_inner // LANES
    pairs_per_group = n_pairs // SSM_N_GROUPS
    L = SSM_CHUNK

    @pl.when(c == 0)
    def _():
        xpad_ref[0:SUBLANES, :] = jnp.zeros((SUBLANES, xpad_ref.shape[1]), F32)
        state_ref[...] = jnp.zeros(state_ref.shape, F32)

    xpad_ref[SUBLANES:SUBLANES + L, :] = xbc_ref[0]
    acc = cb_ref[...] + cw_ref[3:4, :] * xpad_ref[SUBLANES:SUBLANES + L, :]
    for kk in range(SSM_CONV_WIDTH - 1):
        shift = SSM_CONV_WIDTH - 1 - kk
        acc = acc + cw_ref[kk:kk + 1, :] * xpad_ref[SUBLANES - shift:SUBLANES - shift + L, :]
    xpad_ref[0:SUBLANES, :] = xpad_ref[L:L + SUBLANES, :]
    act = _silu(acc)

    row = lax.broadcasted_iota(jnp.int32, (L, L), 0)
    col = lax.broadcasted_iota(jnp.int32, (L, L), 1)
    causal = col <= row
    tri = causal.astype(F32)
    first_head = col < SSM_HEAD_DIM

    dt = _softplus(sm_ref[0][:, :LANES] + dtb_ref[...])
    a_neg = -jnp.exp(alog_ref[...])
    acs = jnp.dot(tri, dt * a_neg, preferred_element_type=F32,
                  precision=lax.Precision.HIGHEST)
    acs_t = acs.T
    dt_t = dt.T

    for g in range(SSM_N_GROUPS):
        bg = act[:, d_inner + g * SSM_D_STATE:d_inner + (g + 1) * SSM_D_STATE]
        cg = act[:, d_inner + d_bc + g * SSM_D_STATE:d_inner + d_bc + (g + 1) * SSM_D_STATE]
        bg_b = bg.astype(BF16)
        cg_b = cg.astype(BF16)
        cb = lax.dot_general(cg_b, bg_b, (((1,), (1,)), ((), ())),
                             preferred_element_type=F32)
        bgt_b = bg.T.astype(BF16)
        for q in range(pairs_per_group):
            p = g * pairs_per_group + q
            xp = act[:, p * LANES:(p + 1) * LANES]
            xp_b = xp.astype(BF16)
            ys = []
            cols = []
            dcols = []
            for e in range(2):
                h = 2 * p + e
                col_b = jnp.broadcast_to(acs[:, h:h + 1], (L, L))
                row_b = jnp.broadcast_to(acs_t[h:h + 1, :], (L, L))
                dtrow_b = jnp.broadcast_to(dt_t[h:h + 1, :], (L, L))
                decay = jnp.where(causal, jnp.exp(col_b - row_b), 0.0)
                mix = (cb * decay * dtrow_b).astype(BF16)
                ys.append(jnp.dot(mix, xp_b, preferred_element_type=F32))
                cols.append(col_b)
                dcols.append(jnp.broadcast_to(dt[:, h:h + 1], (L, L)))
            y_intra = jnp.where(first_head, ys[0], ys[1])
            colsel = jnp.where(first_head, cols[0], cols[1])
            dtsel = jnp.where(first_head, dcols[0], dcols[1])
            st = state_ref[p]
            y_inter = jnp.dot(cg_b, st.astype(BF16), preferred_element_type=F32) * jnp.exp(colsel)
            y_ref[:, p * LANES:(p + 1) * LANES] = (
                y_intra + y_inter + d_ref[:, p * LANES:(p + 1) * LANES] * xp)
            last = colsel[L - 1:L, :]
            xw = (xp * (jnp.exp(last - colsel) * dtsel)).astype(BF16)
            state_ref[p] = st * jnp.exp(last) + jnp.dot(bgt_b, xw, preferred_element_type=F32)

    gw = d_inner // SSM_N_GROUPS
    for g in range(SSM_N_GROUPS):
        sl = slice(g * gw, (g + 1) * gw)
        t = y_ref[:, sl] * _silu(z_ref[0][:, sl])
        ms = jnp.mean(t * t, axis=-1, keepdims=True)
        o_ref[0, :, sl] = (t * lax.rsqrt(ms + RMS_EPS) * ng_ref[:, sl]).astype(o_ref.dtype)


def _ssd_branch(xbc, z, small, conv_w, conv_b, dt_bias, a_log, d_skip, norm_g):
    bsz, seq, conv_dim = xbc.shape
    d_inner = z.shape[2]
    n_heads = d_inner // SSM_HEAD_DIM
    nc = seq // SSM_CHUNK
    pad = LANES - n_heads
    dtb = jnp.pad(dt_bias.astype(F32), (0, pad)).reshape(1, LANES)
    alog = jnp.pad(a_log.astype(F32), (0, pad)).reshape(1, LANES)
    d_row = jnp.repeat(d_skip.astype(F32), SSM_HEAD_DIM).reshape(1, d_inner)
    full = lambda b, c: (0, 0)
    return pl.pallas_call(
        _ssd_kernel,
        grid=(bsz, nc),
        in_specs=[pl.BlockSpec((1, SSM_CHUNK, conv_dim), lambda b, c: (b, c, 0)),
                  pl.BlockSpec((1, SSM_CHUNK, d_inner), lambda b, c: (b, c, 0)),
                  pl.BlockSpec((1, SSM_CHUNK, small.shape[2]), lambda b, c: (b, c, 0)),
                  pl.BlockSpec((SSM_CONV_WIDTH, conv_dim), full),
                  pl.BlockSpec((1, conv_dim), full),
                  pl.BlockSpec((1, LANES), full),
                  pl.BlockSpec((1, LANES), full),
                  pl.BlockSpec((1, d_inner), full),
                  pl.BlockSpec((1, d_inner), full)],
        out_specs=pl.BlockSpec((1, SSM_CHUNK, d_inner), lambda b, c: (b, c, 0)),
        out_shape=jax.ShapeDtypeStruct((bsz, seq, d_inner), BF16),
        scratch_shapes=[pltpu.VMEM((SSM_CHUNK + 2 * SUBLANES, conv_dim), F32),
                        pltpu.VMEM((d_inner // LANES, SSM_D_STATE, LANES), F32),
                        pltpu.VMEM((SSM_CHUNK, d_inner), F32)],
        compiler_params=_params("arbitrary", "arbitrary"),
        name="ssd_scan",
    )(xbc, z, small, conv_w.astype(F32), conv_b.reshape(1, conv_dim).astype(F32),
      dtb, alog, d_row, norm_g.reshape(1, d_inner).astype(F32))


def _head_sum(x, ind_ref, indt_ref):
    s = _split_dot(x, ind_ref[...])
    return _split_dot(s, indt_ref[...])


def _rwkv_prep_kernel(x_ref, xh_ref, s_ref, sh_ref, mu_ref, mus_ref, w0_ref, wup_ref, a0_ref,
                      aup_ref, kk_ref, ka_ref, rk_ref, ind_ref, indt_ref,
                      r_o, lw_o, k_o, v_o, a_o, b_o, bonus_o, gs_o):
    i = pl.program_id(1)
    tm = x_ref.shape[1]
    d = r_o.shape[2]
    not_first = (i > 0).astype(F32)

    def shifted(cur, halo):
        prev_row = halo[SUBLANES - 1:SUBLANES, :] * not_first
        rolled = pltpu.roll(cur, 1, 0)
        rid = lax.broadcasted_iota(jnp.int32, cur.shape, 0)
        return jnp.where(rid == 0, prev_row, rolled)

    x = x_ref[0]
    rw = x + (shifted(x, xh_ref[0]) - x) * mu_ref[...]
    s = s_ref[0]
    rs = s + (shifted(s, sh_ref[0]) - s) * mus_ref[...]

    r = rw[:, 0:d]
    k = rw[:, d:2 * d]
    v = rw[:, 2 * d:3 * d]
    g = rw[:, 3 * d:4 * d]

    w_arg = w0_ref[...] + jnp.dot(jnp.tanh(rs), wup_ref[...], preferred_element_type=F32,
                                  precision=lax.Precision.HIGHEST)
    w_log = -_softplus(-w_arg) - 0.5
    lw_o[0] = -jnp.exp(w_log)
    a = jax.nn.sigmoid(a0_ref[...] + jnp.dot(rs, aup_ref[...], preferred_element_type=F32,
                                             precision=lax.Precision.HIGHEST))

    kk = k * kk_ref[...]
    nrm = jnp.sqrt(_head_sum(kk * kk, ind_ref, indt_ref))
    kk = kk / jnp.maximum(nrm, 1e-12)
    k2 = k * (1.0 + (a - 1.0) * ka_ref[...])
    r_o[0] = r.astype(r_o.dtype)
    k_o[0] = k2.astype(k_o.dtype)
    v_o[0] = v.astype(v_o.dtype)
    a_o[0] = (-kk).astype(a_o.dtype)
    b_o[0] = (kk * a).astype(b_o.dtype)
    bonus_o[0] = _head_sum(r * k2 * rk_ref[...], ind_ref, indt_ref) * v
    gs_o[0] = _silu(g).astype(gs_o.dtype)


def _pick_heads(res, head_masks):
    c = RWKV_CHUNK
    out = res[0:c]
    for h in range(1, UNIT_HEADS):
        out = jnp.where(head_masks[h], res[h * c:(h + 1) * c], out)
    return out


def _rwkv_intra_kernel(r_ref, lw_ref, k_ref, v_ref, a_ref, b_ref,
                       rh_o, ah_o, bh_o, kh_o, yh_o, vh_o, gc_o,
                       q_ref, apl_ref, tpl_ref, aa_ref, ar_ref, at_ref, rt_ref):
    bsz, tb, uw = r_ref.shape
    c = RWKV_CHUNK
    n_c = tb // c
    nt = (((1,), (1,)), ((), ()))

    ti = lax.broadcasted_iota(jnp.int32, (c, c), 0)
    si = lax.broadcasted_iota(jnp.int32, (c, c), 1)
    tri = (si <= ti).astype(F32)
    rowt = lax.broadcasted_iota(jnp.int32, (UNIT_HEADS * c, 2 * c), 0) % c
    cols = lax.broadcasted_iota(jnp.int32, (UNIT_HEADS * c, 2 * c), 1) % c
    strict = cols < rowt
    incl = cols <= rowt
    lane_head = lax.broadcasted_iota(jnp.int32, (c, uw), 1) // RWKV_HEAD_DIM
    head_masks = [lane_head == h for h in range(UNIT_HEADS)]
    first_half = lax.broadcasted_iota(jnp.int32, (c, 2 * c), 1) < c
    f32 = lambda ref, b, rows: ref[b, rows, :].astype(F32)

    for b in range(bsz):
        def stage_a(ci, carry, b=b):
            j = b * n_c + ci
            rows = pl.ds(pl.multiple_of(ci * c, c), c)
            lw = lw_ref[b, rows, :]
            cs = jnp.dot(tri, lw, preferred_element_type=F32, precision=lax.Precision.HIGHEST)
            g = jnp.exp(cs)
            gi = jnp.exp(-cs)
            at = f32(a_ref, b, rows) * jnp.exp(cs - lw)
            rt = f32(r_ref, b, rows) * g
            bt = f32(b_ref, b, rows) * gi
            kt = f32(k_ref, b, rows) * gi
            gc = g[c - 1:c, :]
            bh_o[b, rows, :] = (bt * gc).astype(bh_o.dtype)
            kh_o[b, rows, :] = (kt * gc).astype(kh_o.dtype)
            gc_o[b, ci] = jnp.broadcast_to(gc, (SUBLANES, uw))
            lhs = jnp.concatenate([jnp.where(m, at, 0.0) for m in head_masks]
                                  + [jnp.where(m, rt, 0.0) for m in head_masks],
                                  axis=0).astype(BF16)
            rhs = jnp.concatenate([bt, kt], axis=0).astype(BF16)
            prod = lax.dot_general(lhs, rhs, nt, preferred_element_type=F32)
            da = jnp.where(strict, prod[0:UNIT_HEADS * c], 0.0)
            dr = jnp.where(incl, prod[UNIT_HEADS * c:], 0.0)
            aa_ref[j] = da.astype(BF16)
            ar_ref[j] = dr.astype(BF16)
            for h in range(UNIT_HEADS):
                q_ref[pl.ds(UNIT_HEADS * j + h, c, stride=Q_STRIDE), :] = jnp.where(
                    first_half, da[h * c:(h + 1) * c], 0.0)
            at_ref[j] = at.astype(BF16)
            rt_ref[j] = rt
            return carry
        lax.fori_loop(0, n_c, stage_a, 0, unroll=4)

    for t in range(c):
        tile = q_ref[t * Q_STRIDE:t * Q_STRIDE + LANES, :]
        apl_ref[t] = tile.T[0:c, :]
    sub = lax.broadcasted_iota(jnp.int32, (SUBLANES, LANES), 0)
    zero8 = jnp.zeros((SUBLANES, LANES), F32)
    for t in range(c):
        nb = t // SUBLANES + 1
        acc = [[None, None] for _ in range(nb)]
        for s in range(t):
            coef = apl_ref[t, s:s + 1, :]
            for jv in range(s // SUBLANES + 1):
                term = coef * tpl_ref[s, jv * SUBLANES:(jv + 1) * SUBLANES, :]
                slot = acc[jv]
                slot[s % 2] = term if slot[s % 2] is None else slot[s % 2] + term
        for jv in range(c // SUBLANES):
            if jv < nb:
                parts = [p for p in acc[jv] if p is not None]
                if jv == nb - 1:
                    parts.append(jnp.where(sub == t % SUBLANES, 1.0, 0.0))
                val = parts[0]
                for p in parts[1:]:
                    val = val + p
            else:
                val = zero8
            tpl_ref[t, jv * SUBLANES:(jv + 1) * SUBLANES, :] = val
    zpad = jnp.zeros((LANES - c, LANES), F32)
    for t in range(c):
        q_ref[t * Q_STRIDE:t * Q_STRIDE + LANES, :] = jnp.concatenate([tpl_ref[t], zpad], axis=0).T

    for b in range(bsz):
        def stage_c(ci, carry, b=b):
            j = b * n_c + ci
            rows = pl.ds(pl.multiple_of(ci * c, c), c)
            tst = jnp.concatenate(
                [q_ref[pl.ds(UNIT_HEADS * j + h, c, stride=Q_STRIDE), :][:, 0:c]
                 for h in range(UNIT_HEADS)], axis=0).astype(BF16)
            v = v_ref[b, rows, :]
            zb = jnp.zeros((c, uw), BF16)
            akv = _pick_heads(jnp.dot(aa_ref[j], jnp.concatenate([zb, v], axis=0),
                                      preferred_element_type=F32), head_masks)
            x4 = jnp.dot(tst, jnp.concatenate([at_ref[j], akv.astype(BF16)], axis=1),
                         preferred_element_type=F32)
            ah = _pick_heads(x4[:, 0:uw], head_masks)
            vh = _pick_heads(x4[:, uw:], head_masks)
            top = jnp.concatenate([ah.astype(BF16), vh.astype(BF16)], axis=1)
            bot = jnp.concatenate([zb, v], axis=1)
            y4 = jnp.dot(ar_ref[j], jnp.concatenate([top, bot], axis=0),
                         preferred_element_type=F32)
            rh_o[b, rows, :] = (rt_ref[j] + _pick_heads(y4[:, 0:uw], head_masks)).astype(rh_o.dtype)
            ah_o[b, rows, :] = ah.astype(ah_o.dtype)
            yh_o[b, rows, :] = _pick_heads(y4[:, uw:], head_masks)
            vh_o[b, rows, :] = vh
            return carry
        lax.fori_loop(0, n_c, stage_c, 0, unroll=4)


def _rwkv_seq_kernel(rh_ref, ah_ref, bh_ref, kh_ref, v_ref, yh_ref, vh_ref, gc_ref, y_o, s_ref):
    bsz, tb, d = rh_ref.shape
    c = RWKV_CHUNK
    uw = UNIT_W
    n_units = d // uw
    nt = (((1,), (1,)), ((), ()))
    tn = (((0,), (0,)), ((), ()))

    @pl.when(pl.program_id(0) == 0)
    def _():
        s_ref[...] = jnp.zeros(s_ref.shape, F32)

    rid = lax.broadcasted_iota(jnp.int32, (uw, uw), 0) // RWKV_HEAD_DIM
    cid = lax.broadcasted_iota(jnp.int32, (uw, uw), 1) // RWKV_HEAD_DIM
    same_head = rid == cid
    units = [(b, u) for b in range(bsz) for u in range(n_units)]

    for ci in range(tb // c):
        rows = slice(ci * c, (ci + 1) * c)
        prods = []
        for i, (b, u) in enumerate(units):
            ul = slice(u * uw, (u + 1) * uw)
            lhs = jnp.concatenate([rh_ref[b, rows, ul], ah_ref[b, rows, ul]], axis=0)
            prods.append(lax.dot_general(lhs, s_ref[i].astype(BF16), nt,
                                         preferred_element_type=F32))
        us = []
        for i, (b, u) in enumerate(units):
            ul = slice(u * uw, (u + 1) * uw)
            yu = jnp.concatenate([yh_ref[b, rows, ul], vh_ref[b, rows, ul]], axis=0) + prods[i]
            y_o[b, rows, ul] = yu[0:c]
            us.append(yu[c:].astype(BF16))
        for i, (b, u) in enumerate(units):
            ul = slice(u * uw, (u + 1) * uw)
            lhs = jnp.concatenate([us[i], v_ref[b, rows, ul]], axis=0)
            rhs = jnp.concatenate([bh_ref[b, rows, ul], kh_ref[b, rows, ul]], axis=0)
            upd = lax.dot_general(lhs, rhs, tn, preferred_element_type=F32)
            s_ref[i] = s_ref[i] * gc_ref[b, ci, 0:1, ul] + jnp.where(same_head, upd, 0.0)


def _rwkv_post_kernel(y_ref, bonus_ref, gs_ref, gg_ref, gb_ref, ind_ref, indt_ref, o_ref):
    y = y_ref[...]
    inv = 1.0 / RWKV_HEAD_DIM
    mu = _head_sum(y, ind_ref, indt_ref) * inv
    dlt = y - mu
    var = _head_sum(dlt * dlt, ind_ref, indt_ref) * inv
    o = dlt * lax.rsqrt(var + RWKV_GN_EPS) * gg_ref[...] + gb_ref[...]
    o_ref[...] = ((o + bonus_ref[...]) * gs_ref[...].astype(F32)).astype(o_ref.dtype)


def _rwkv_branch(rkvg, small, mu, w0, w_up, a0, a_up, k_k, k_a, r_k, gn_g, gn_b):
    bsz, seq, d4 = rkvg.shape
    d = d4 // 4
    n_heads = d // RWKV_HEAD_DIM
    sw = small.shape[2]
    lo = sw - 2 * RWKV_LORA
    mu = mu.astype(F32)
    mu_main = mu[:4 * d].reshape(1, 4 * d)
    mu_small = jnp.pad(mu[4 * d:], (lo, 0)).reshape(1, sw)
    wup_pad = jnp.pad(w_up.astype(F32), ((lo, RWKV_LORA), (0, 0)))
    aup_pad = jnp.pad(a_up.astype(F32), ((lo + RWKV_LORA, 0), (0, 0)))
    head_of = jnp.arange(d) // RWKV_HEAD_DIM
    ind = (head_of[:, None] == jnp.arange(LANES)[None, :]).astype(BF16)
    indt = ind.T
    vec = lambda t: t.reshape(1, d).astype(F32)

    tm = 128
    nt = seq // tm
    hb = tm // SUBLANES
    cur = lambda w: pl.BlockSpec((1, tm, w), lambda b, i: (b, i, 0))
    halo = lambda w: pl.BlockSpec((1, SUBLANES, w), lambda b, i: (b, jnp.maximum(i * hb - 1, 0), 0))
    full = lambda s0, s1: pl.BlockSpec((s0, s1), lambda b, i: (0, 0))
    outs = pl.pallas_call(
        _rwkv_prep_kernel,
        grid=(bsz, nt),
        in_specs=[cur(4 * d), halo(4 * d), cur(sw), halo(sw), full(1, 4 * d), full(1, sw),
                  full(1, d), full(sw, d), full(1, d), full(sw, d), full(1, d), full(1, d),
                  full(1, d), full(d, LANES), full(LANES, d)],
        out_specs=[cur(d)] * 8,
        out_shape=[jax.ShapeDtypeStruct((bsz, seq, d), dt)
                   for dt in (BF16, F32, BF16, BF16, BF16, BF16, F32, BF16)],
        compiler_params=_params("parallel", "parallel"),
        name="rwkv_prep",
    )(rkvg, rkvg, small, small, mu_main, mu_small, vec(w0), wup_pad, vec(a0), aup_pad,
      vec(k_k), vec(k_a), vec(r_k), ind, indt)
    r, lw, k, v, na, kb, bonus, gs = outs

    c = RWKV_CHUNK
    uw = UNIT_W
    n_c = LANES // (bsz * UNIT_HEADS)
    tb = n_c * c
    n_prob = bsz * n_c
    blk = pl.BlockSpec((bsz, tb, uw), lambda i, u: (0, i, u))
    shp = lambda dt: jax.ShapeDtypeStruct((bsz, seq, d), dt)
    rh, ah, bh, kh, yh, vh, gc = pl.pallas_call(
        _rwkv_intra_kernel,
        grid=(seq // tb, d // uw),
        in_specs=[blk] * 6,
        out_specs=[blk] * 6 + [pl.BlockSpec((bsz, n_c, SUBLANES, uw), lambda i, u: (0, i, 0, u))],
        out_shape=[shp(BF16), shp(BF16), shp(BF16), shp(BF16), shp(F32), shp(F32),
                   jax.ShapeDtypeStruct((bsz, seq // c, SUBLANES, d), F32)],
        scratch_shapes=[pltpu.VMEM((c * Q_STRIDE, LANES), F32),
                        pltpu.VMEM((c, c, LANES), F32),
                        pltpu.VMEM((c, c, LANES), F32),
                        pltpu.VMEM((n_prob, UNIT_HEADS * c, 2 * c), BF16),
                        pltpu.VMEM((n_prob, UNIT_HEADS * c, 2 * c), BF16),
                        pltpu.VMEM((n_prob, c, uw), BF16),
                        pltpu.VMEM((n_prob, c, uw), F32)],
        compiler_params=_params("parallel", "parallel"),
        name="rwkv_intra",
    )(r, lw, k, v, na, kb)

    ts = 2 * c
    sblk = pl.BlockSpec((bsz, ts, d), lambda i: (0, i, 0))
    y = pl.pallas_call(
        _rwkv_seq_kernel,
        grid=(seq // ts,),
        in_specs=[sblk] * 7 + [pl.BlockSpec((bsz, ts // c, SUBLANES, d), lambda i: (0, i, 0, 0))],
        out_specs=sblk,
        out_shape=shp(F32),
        scratch_shapes=[pltpu.VMEM((bsz * d // uw, uw, uw), F32)],
        compiler_params=_params("arbitrary"),
        name="rwkv_seq",
    )(rh, ah, bh, kh, v, yh, vh, gc)

    m = bsz * seq
    tp = 256
    rows = lambda: pl.BlockSpec((tp, d), lambda i: (i, 0))
    c1 = lambda s0, s1: pl.BlockSpec((s0, s1), lambda i: (0, 0))
    return pl.pallas_call(
        _rwkv_post_kernel,
        grid=(m // tp,),
        in_specs=[rows(), rows(), rows(), c1(1, d), c1(1, d), c1(d, LANES), c1(LANES, d)],
        out_specs=rows(),
        out_shape=jax.ShapeDtypeStruct((m, d), BF16),
        compiler_params=_params("parallel"),
        name="rwkv_post",
    )(y.reshape(m, d), bonus.reshape(m, d), gs.reshape(m, d), vec(gn_g), vec(gn_b), ind, indt)


def _mem_attn_kernel(q_ref, k_ref, v_ref, g_ref, o_ref):
    d = q_ref.shape[2]
    hd = d // MEM_HEADS
    scale = hd ** -0.5
    for h in range(MEM_HEADS):
        sl = slice(h * hd, (h + 1) * hd)
        s = lax.dot_general(q_ref[0, :, sl], k_ref[0, :, sl], (((1,), (1,)), ((), ())),
                            preferred_element_type=F32) * scale
        s = s - jnp.max(s, axis=-1, keepdims=True)
        e = jnp.exp(s)
        p = e / jnp.sum(e, axis=-1, keepdims=True)
        om = jnp.dot(p.astype(BF16), v_ref[0, :, sl], preferred_element_type=F32)
        o_ref[0, :, sl] = (om * _silu(g_ref[0, :, sl])).astype(o_ref.dtype)


def _mem_branch(q, gate, mem_k, mem_v):
    bsz, seq, d = q.shape
    n_mem = mem_k.shape[1]
    tm = 512
    cur = pl.BlockSpec((1, tm, d), lambda b, i: (b, i, 0))
    kv = pl.BlockSpec((1, n_mem, d), lambda b, i: (b, 0, 0))
    return pl.pallas_call(
        _mem_attn_kernel,
        grid=(bsz, seq // tm),
        in_specs=[cur, kv, kv, cur],
        out_specs=cur,
        out_shape=jax.ShapeDtypeStruct((bsz, seq, d), BF16),
        compiler_params=_params("parallel", "parallel"),
        name="mem_attn",
    )(q, mem_k, mem_v, gate)


def _merge_kernel(a1_ref, a2_ref, a3_ref, w1_ref, w2_ref, w3_ref, g1_ref, g2_ref, g3_ref,
                  b1_ref, b2_ref, b3_ref, o_ref):
    def term(a_ref, w_ref, g_ref, b_ref):
        y = jnp.dot(a_ref[...], w_ref[...], preferred_element_type=F32)
        return jax.nn.sigmoid(g_ref[...] + b_ref[...]) * y

    acc = term(a1_ref, w1_ref, g1_ref, b1_ref)
    acc = acc + term(a2_ref, w2_ref, g2_ref, b2_ref)
    acc = acc + term(a3_ref, w3_ref, g3_ref, b3_ref)
    o_ref[...] = acc.astype(o_ref.dtype)


def _merge(a_ssm, a_rwkv, a_mem, w_ssm, w_rwkv, w_mem, gate_raw, gate_b):
    m, d = a_rwkv.shape
    tm, tn = 512, 512
    nb = d // tn
    act = lambda a: pl.BlockSpec((tm, a.shape[1]), lambda j, i: (i, 0))
    wgt = lambda w: pl.BlockSpec((w.shape[0], tn), lambda j, i: (0, j))
    gate = lambda k: pl.BlockSpec((tm, tn), lambda j, i: (i, j + k * nb))
    bias = lambda k: pl.BlockSpec((1, tn), lambda j, i: (0, j + k * nb))
    gb = gate_b.reshape(1, 3 * d).astype(F32)
    return pl.pallas_call(
        _merge_kernel,
        grid=(nb, m // tm),
        in_specs=[act(a_ssm), act(a_rwkv), act(a_mem), wgt(w_ssm), wgt(w_rwkv), wgt(w_mem),
                  gate(0), gate(1), gate(2), bias(0), bias(1), bias(2)],
        out_specs=pl.BlockSpec((tm, tn), lambda j, i: (i, j)),
        out_shape=jax.ShapeDtypeStruct((m, d), BF16),
        compiler_params=_params("parallel", "parallel"),
        name="merge",
    )(a_ssm, a_rwkv, a_mem, w_ssm, w_rwkv, w_mem, gate_raw, gate_raw, gate_raw, gb, gb, gb)


def _out_kernel(m_ref, w_ref, g_ref, x_ref, o_ref):
    y = jnp.dot(m_ref[...], w_ref[...], preferred_element_type=F32)
    ms = jnp.mean(y * y, axis=-1, keepdims=True)
    o_ref[...] = x_ref[...] + y * lax.rsqrt(ms + RMS_EPS) * g_ref[...]


def _out_proj(merged, w_out, g_post, x2d):
    m, d = x2d.shape
    tm = 512
    rows = pl.BlockSpec((tm, d), lambda i: (i, 0))
    return pl.pallas_call(
        _out_kernel,
        grid=(m // tm,),
        in_specs=[rows, pl.BlockSpec((d, d), lambda i: (0, 0)),
                  pl.BlockSpec((1, d), lambda i: (0, 0)), rows],
        out_specs=rows,
        out_shape=jax.ShapeDtypeStruct((m, d), F32),
        compiler_params=_params("parallel"),
        name="out_proj",
    )(merged, w_out, g_post.reshape(1, d).astype(F32), x2d)


def _layer(h, mem, g_pre, w_in, ssm_conv_w, ssm_conv_b, ssm_dt_bias, ssm_a_log, ssm_d, ssm_norm_g,
           rwkv_mu, rwkv_w0, rwkv_w_up, rwkv_a0, rwkv_a_up, rwkv_k_k, rwkv_k_a, rwkv_r_k,
           rwkv_gn_g, rwkv_gn_b, mem_norm_g, mem_w_kv, gate_b, w_br_ssm, w_br_rwkv, w_br_mem,
           w_out, g_post):
    bsz, seq, d = h.shape
    m = bsz * seq
    d_inner = w_br_ssm.shape[0]
    n_ssm_heads = d_inner // SSM_HEAD_DIM
    conv_dim = ssm_conv_w.shape[1]
    d_rwkv = w_br_rwkv.shape[0]
    d_mem = w_br_mem.shape[0]
    x2d = h.reshape(m, d)

    o_z = 0
    o_xbc = o_z + d_inner
    o_dt = o_xbc + conv_dim
    o_rw = o_dt + n_ssm_heads
    o_lora = o_rw + 4 * d_rwkv
    o_q = o_lora + 2 * RWKV_LORA
    o_mg = o_q + d_mem
    o_gate = o_mg + d_mem
    wb = lambda a, b: w_in[:, a:b].astype(BF16)
    w_small = jnp.concatenate([w_in[:, o_dt:o_rw], w_in[:, o_lora:o_q]], axis=1).astype(BF16)

    u = _rmsnorm(x2d, g_pre, BF16, 512)
    z = _matmul(u, wb(o_z, o_xbc), F32, 1024, 1024, "proj_z")
    xbc = _matmul(u, wb(o_xbc, o_dt), F32, 1024, 1024, "proj_xbc")
    rkvg = _matmul(u, wb(o_rw, o_lora), F32, 1024, 1024, "proj_rwkv")
    small = _matmul(u, w_small, F32, 1024, w_small.shape[1], "proj_small")
    mem_q = _matmul(u, wb(o_q, o_mg), BF16, 1024, 1024, "proj_memq")
    mem_g = _matmul(u, wb(o_mg, o_gate), F32, 1024, 1024, "proj_memg")
    gate_raw = _matmul(u, wb(o_gate, w_in.shape[1]), F32, 1024, 1024, "proj_gate")

    a_ssm = _ssd_branch(xbc.reshape(bsz, seq, conv_dim), z.reshape(bsz, seq, d_inner),
                        small.reshape(bsz, seq, -1), ssm_conv_w, ssm_conv_b, ssm_dt_bias,
                        ssm_a_log, ssm_d, ssm_norm_g).reshape(m, d_inner)

    a_rwkv = _rwkv_branch(rkvg.reshape(bsz, seq, 4 * d_rwkv), small.reshape(bsz, seq, -1),
                          rwkv_mu, rwkv_w0, rwkv_w_up, rwkv_a0, rwkv_a_up, rwkv_k_k, rwkv_k_a,
                          rwkv_r_k, rwkv_gn_g, rwkv_gn_b)

    n_mem = mem.shape[1]
    mem_n = _rmsnorm(mem.reshape(bsz * n_mem, d), mem_norm_g, BF16, n_mem)
    mem_kv = _matmul(mem_n, mem_w_kv.astype(BF16), BF16, bsz * n_mem, 1024, "proj_memkv")
    mem_k = mem_kv[:, :d_mem].reshape(bsz, n_mem, d_mem)
    mem_v = mem_kv[:, d_mem:].reshape(bsz, n_mem, d_mem)
    a_mem = _mem_branch(mem_q.reshape(bsz, seq, d_mem), mem_g.reshape(bsz, seq, d_mem),
                        mem_k, mem_v).reshape(m, d_mem)

    merged = _merge(a_ssm, a_rwkv, a_mem, w_br_ssm.astype(BF16), w_br_rwkv.astype(BF16),
                    w_br_mem.astype(BF16), gate_raw, gate_b)
    out = _out_proj(merged, w_out.astype(BF16), g_post, x2d)
    return out.reshape(bsz, seq, d)


def kernel(x, mem, g_pre, w_in, ssm_conv_w, ssm_conv_b, ssm_dt_bias, ssm_a_log, ssm_d, ssm_norm_g, rwkv_mu, rwkv_w0, rwkv_w_up, rwkv_a0, rwkv_a_up, rwkv_k_k, rwkv_k_a, rwkv_r_k, rwkv_gn_g, rwkv_gn_b, mem_norm_g, mem_w_kv, gate_b, w_br_ssm, w_br_rwkv, w_br_mem, w_out, g_post):
    layer_params = (g_pre, w_in, ssm_conv_w, ssm_conv_b, ssm_dt_bias, ssm_a_log, ssm_d, ssm_norm_g,
                    rwkv_mu, rwkv_w0, rwkv_w_up, rwkv_a0, rwkv_a_up, rwkv_k_k, rwkv_k_a, rwkv_r_k,
                    rwkv_gn_g, rwkv_gn_b, mem_norm_g, mem_w_kv, gate_b, w_br_ssm, w_br_rwkv,
                    w_br_mem, w_out, g_post)
    h = x
    for layer in range(g_pre.shape[0]):
        h = _layer(h, mem, *[p[layer] for p in layer_params])
    return h
```

```python
import functools

import jax
import jax.numpy as jnp
from jax import lax
from jax.experimental import pallas as pl
from jax.experimental.pallas import tpu as pltpu

F32 = jnp.float32
BF16 = jnp.bfloat16

RMS_EPS = 1e-6
LOG2E = 1.4426950408889634
SSM_HEAD_DIM = 64
SSM_N_GROUPS = 8
SSM_D_STATE = 128
SSM_CONV_WIDTH = 4
SSM_CHUNK = 128
RWKV_HEAD_DIM = 64
RWKV_LORA = 96
RWKV_GN_EPS = 64e-5
RWKV_CHUNK = 64
UNIT_HEADS = 4
UNIT_W = UNIT_HEADS * RWKV_HEAD_DIM
INTRA_GROUP = 8
MEM_HEADS = 4

LANES = 128
SUBLANES = 8
MXU_N = 256
VMEM_LIMIT = 56 * 1024 * 1024
Q_STRIDE = LANES + SUBLANES


def _params(*sem):
    return pltpu.CompilerParams(dimension_semantics=sem, vmem_limit_bytes=VMEM_LIMIT)


def _silu(x):
    return x * jax.nn.sigmoid(x)


def _softplus(x):
    return jnp.maximum(x, 0.0) + jnp.log(1.0 + jnp.exp(-jnp.abs(x)))


def _split_dot(x, w_bf16):
    hi = x.astype(BF16)
    lo = (x - hi.astype(F32)).astype(BF16)
    return (jnp.dot(hi, w_bf16, preferred_element_type=F32)
            + jnp.dot(lo, w_bf16, preferred_element_type=F32))


def _cumsum_rows(tri_bf16, x):
    hi = x.astype(BF16)
    r1 = x - hi.astype(F32)
    mid = r1.astype(BF16)
    lo = (r1 - mid.astype(F32)).astype(BF16)
    dot = lambda p: jnp.dot(tri_bf16, p, preferred_element_type=F32)
    return dot(hi) + (dot(mid) + dot(lo))


def _dot_hi_lo(x, w_hi_ref, w_lo_ref):
    hi = x.astype(BF16)
    lo = (x - hi.astype(F32)).astype(BF16)
    w_hi = w_hi_ref[...]
    return (jnp.dot(hi, w_hi, preferred_element_type=F32)
            + (jnp.dot(hi, w_lo_ref[...], preferred_element_type=F32)
               + jnp.dot(lo, w_hi, preferred_element_type=F32)))


def _rmsnorm_kernel(x_ref, g_ref, o_ref):
    x = x_ref[...].astype(F32)
    ms = jnp.mean(x * x, axis=-1, keepdims=True)
    o_ref[...] = (x * lax.rsqrt(ms + RMS_EPS) * g_ref[...]).astype(o_ref.dtype)


def _rmsnorm(x2d, g, out_dtype, tm):
    m, d = x2d.shape
    return pl.pallas_call(
        _rmsnorm_kernel,
        grid=(m // tm,),
        in_specs=[pl.BlockSpec((tm, d), lambda i: (i, 0)),
                  pl.BlockSpec((1, d), lambda i: (0, 0))],
        out_specs=pl.BlockSpec((tm, d), lambda i: (i, 0)),
        out_shape=jax.ShapeDtypeStruct((m, d), out_dtype),
        compiler_params=_params("parallel"),
        name="rmsnorm",
    )(x2d, g.reshape(1, d).astype(F32))


def _mm_kernel(a_ref, w_ref, o_ref):
    o_ref[...] = jnp.dot(a_ref[...], w_ref[...],
                         preferred_element_type=F32).astype(o_ref.dtype)


def _matmul(a, w, out_dtype, tm, tn, name):
    m, k = a.shape
    n = w.shape[1]
    tm = min(tm, m)
    tn = min(tn, n)
    return pl.pallas_call(
        _mm_kernel,
        grid=(n // tn, m // tm),
        in_specs=[pl.BlockSpec((tm, k), lambda j, i: (i, 0)),
                  pl.BlockSpec((k, tn), lambda j, i: (0, j))],
        out_specs=pl.BlockSpec((tm, tn), lambda j, i: (i, j)),
        out_shape=jax.ShapeDtypeStruct((m, n), out_dtype),
        compiler_params=_params("parallel", "parallel"),
        name=name,
    )(a, w)


def _mm_silu_kernel(a_ref, w_ref, o_ref):
    a = a_ref[...]
    for n in range(o_ref.shape[1] // MXU_N):
        sl = slice(n * MXU_N, (n + 1) * MXU_N)
        acc = jnp.dot(a, w_ref[:, sl], preferred_element_type=F32)
        o_ref[:, sl] = _silu(acc).astype(o_ref.dtype)


def _matmul_silu(a, w, out_dtype, tm, tn, name):
    m, k = a.shape
    n = w.shape[1]
    return pl.pallas_call(
        _mm_silu_kernel,
        grid=(n // tn, m // tm),
        in_specs=[pl.BlockSpec((tm, k), lambda j, i: (i, 0)),
                  pl.BlockSpec((k, tn), lambda j, i: (0, j))],
        out_specs=pl.BlockSpec((tm, tn), lambda j, i: (i, j)),
        out_shape=jax.ShapeDtypeStruct((m, n), out_dtype),
        compiler_params=_params("parallel", "parallel"),
        name=name,
    )(a, w)


def _ssd_kernel(xbc_ref, zs_ref, sm_ref, cw_ref, cb_ref, dtb_ref, alog_ref, d_ref, ng_ref,
                o_ref, xpad_ref, state_ref, y_ref):
    c = pl.program_id(1)
    d_inner = zs_ref.shape[2]
    d_bc = SSM_N_GROUPS * SSM_D_STATE
    n_pairs = d_inner // LANES
    pairs_per_group = n_pairs // SSM_N_GROUPS
    L = SSM_CHUNK

    @pl.when(c == 0)
    def _():
        xpad_ref[0:SUBLANES, :] = jnp.zeros((SUBLANES, xpad_ref.shape[1]), F32)
        state_ref[...] = jnp.zeros(state_ref.shape, F32)

    xpad_ref[SUBLANES:SUBLANES + L, :] = xbc_ref[0]
    acc = cb_ref[...] + cw_ref[3:4, :] * xpad_ref[SUBLANES:SUBLANES + L, :]
    for kk in range(SSM_CONV_WIDTH - 1):
        shift = SSM_CONV_WIDTH - 1 - kk
        acc = acc + cw_ref[kk:kk + 1, :] * xpad_ref[SUBLANES - shift:SUBLANES - shift + L, :]
    xpad_ref[0:SUBLANES, :] = xpad_ref[L:L + SUBLANES, :]
    act = _silu(acc)

    row = lax.broadcasted_iota(jnp.int32, (L, L), 0)
    col = lax.broadcasted_iota(jnp.int32, (L, L), 1)
    causal = col <= row
    tri = causal.astype(BF16)
    first_head = col < SSM_HEAD_DIM

    dt = _softplus(sm_ref[0][:, :LANES] + dtb_ref[...])
    a_neg = -jnp.exp(alog_ref[...]) * LOG2E
    acs = _cumsum_rows(tri, dt * a_neg)
    acs_t = acs.T
    dt_t = dt.T

    for g in range(SSM_N_GROUPS):
        bg = act[:, d_inner + g * SSM_D_STATE:d_inner + (g + 1) * SSM_D_STATE]
        cg = act[:, d_inner + d_bc + g * SSM_D_STATE:d_inner + d_bc + (g + 1) * SSM_D_STATE]
        bg_b = bg.astype(BF16)
        cg_b = cg.astype(BF16)
        cb = lax.dot_general(cg_b, bg_b, (((1,), (1,)), ((), ())),
                             preferred_element_type=F32)
        bgt_b = bg.T.astype(BF16)
        for q in range(pairs_per_group):
            p = g * pairs_per_group + q
            xp = act[:, p * LANES:(p + 1) * LANES]
            xp_b = xp.astype(BF16)
            ys = []
            cols = []
            dcols = []
            for e in range(2):
                h = 2 * p + e
                col_b = jnp.broadcast_to(acs[:, h:h + 1], (L, L))
                row_b = jnp.broadcast_to(acs_t[h:h + 1, :], (L, L))
                dtrow_b = jnp.broadcast_to(dt_t[h:h + 1, :], (L, L))
                decay = jnp.where(causal, jnp.exp2(col_b - row_b), 0.0)
                mix = (cb * decay * dtrow_b).astype(BF16)
                ys.append(jnp.dot(mix, xp_b, preferred_element_type=F32))
                cols.append(col_b)
                dcols.append(jnp.broadcast_to(dt[:, h:h + 1], (L, L)))
            y_intra = jnp.where(first_head, ys[0], ys[1])
            colsel = jnp.where(first_head, cols[0], cols[1])
            dtsel = jnp.where(first_head, dcols[0], dcols[1])
            st = state_ref[p]
            y_inter = jnp.dot(cg_b, st.astype(BF16), preferred_element_type=F32) * jnp.exp2(colsel)
            y_ref[:, p * LANES:(p + 1) * LANES] = (
                y_intra + y_inter + d_ref[:, p * LANES:(p + 1) * LANES] * xp)
            last = colsel[L - 1:L, :]
            xw = (xp * (jnp.exp2(last - colsel) * dtsel)).astype(BF16)
            state_ref[p] = st * jnp.exp2(last) + jnp.dot(bgt_b, xw, preferred_element_type=F32)

    gw = d_inner // SSM_N_GROUPS
    for g in range(SSM_N_GROUPS):
        sl = slice(g * gw, (g + 1) * gw)
        t = y_ref[:, sl] * zs_ref[0, :, sl].astype(F32)
        ms = jnp.mean(t * t, axis=-1, keepdims=True)
        o_ref[0, :, sl] = (t * lax.rsqrt(ms + RMS_EPS) * ng_ref[:, sl]).astype(o_ref.dtype)


def _ssd_branch(xbc, zs, small, conv_w, conv_b, dt_bias, a_log, d_skip, norm_g):
    bsz, seq, conv_dim = xbc.shape
    d_inner = zs.shape[2]
    n_heads = d_inner // SSM_HEAD_DIM
    nc = seq // SSM_CHUNK
    pad = LANES - n_heads
    dtb = jnp.pad(dt_bias.astype(F32), (0, pad)).reshape(1, LANES)
    alog = jnp.pad(a_log.astype(F32), (0, pad)).reshape(1, LANES)
    d_row = jnp.repeat(d_skip.astype(F32), SSM_HEAD_DIM).reshape(1, d_inner)
    full = lambda b, c: (0, 0)
    return pl.pallas_call(
        _ssd_kernel,
        grid=(bsz, nc),
        in_specs=[pl.BlockSpec((1, SSM_CHUNK, conv_dim), lambda b, c: (b, c, 0)),
                  pl.BlockSpec((1, SSM_CHUNK, d_inner), lambda b, c: (b, c, 0)),
                  pl.BlockSpec((1, SSM_CHUNK, small.shape[2]), lambda b, c: (b, c, 0)),
                  pl.BlockSpec((SSM_CONV_WIDTH, conv_dim), full),
                  pl.BlockSpec((1, conv_dim), full),
                  pl.BlockSpec((1, LANES), full),
                  pl.BlockSpec((1, LANES), full),
                  pl.BlockSpec((1, d_inner), full),
                  pl.BlockSpec((1, d_inner), full)],
        out_specs=pl.BlockSpec((1, SSM_CHUNK, d_inner), lambda b, c: (b, c, 0)),
        out_shape=jax.ShapeDtypeStruct((bsz, seq, d_inner), BF16),
        scratch_shapes=[pltpu.VMEM((SSM_CHUNK + 2 * SUBLANES, conv_dim), F32),
                        pltpu.VMEM((d_inner // LANES, SSM_D_STATE, LANES), F32),
                        pltpu.VMEM((SSM_CHUNK, d_inner), F32)],
        compiler_params=_params("arbitrary", "arbitrary"),
        name="ssd_scan",
    )(xbc, zs, small, conv_w.astype(F32), conv_b.reshape(1, conv_dim).astype(F32),
      dtb, alog, d_row, norm_g.reshape(1, d_inner).astype(F32))


def _head_sum(x, ind_ref, indt_ref):
    s = _split_dot(x, ind_ref[...])
    return _split_dot(s, indt_ref[...])


def _rwkv_prep_kernel(x_ref, xh_ref, s_ref, sh_ref, mu_ref, mus_ref, w0_ref, wup_ref, wupl_ref,
                      a0_ref, aup_ref, aupl_ref, kk_ref, ka_ref, rk_ref, ind_ref, indt_ref,
                      r_o, lw_o, k_o, v_o, a_o, b_o, bonus_o, gs_o):
    i = pl.program_id(1)
    tm = x_ref.shape[1]
    d = r_o.shape[2]
    not_first = (i > 0).astype(F32)

    def shifted(cur, halo):
        prev_row = halo[SUBLANES - 1:SUBLANES, :] * not_first
        rolled = pltpu.roll(cur, 1, 0)
        rid = lax.broadcasted_iota(jnp.int32, cur.shape, 0)
        return jnp.where(rid == 0, prev_row, rolled)

    x = x_ref[0]
    rw = x + (shifted(x, xh_ref[0]) - x) * mu_ref[...]
    s = s_ref[0]
    rs = s + (shifted(s, sh_ref[0]) - s) * mus_ref[...]

    r = rw[:, 0:d]
    k = rw[:, d:2 * d]
    v = rw[:, 2 * d:3 * d]
    g = rw[:, 3 * d:4 * d]

    w_arg = w0_ref[...] + _dot_hi_lo(jnp.tanh(rs), wup_ref, wupl_ref)
    w_log = -_softplus(-w_arg) - 0.5
    lw_o[0] = -jnp.exp(w_log)
    a = jax.nn.sigmoid(a0_ref[...] + _dot_hi_lo(rs, aup_ref, aupl_ref))

    kk = k * kk_ref[...]
    nrm = jnp.sqrt(_head_sum(kk * kk, ind_ref, indt_ref))
    kk = kk / jnp.maximum(nrm, 1e-12)
    k2 = k * (1.0 + (a - 1.0) * ka_ref[...])
    r_o[0] = r.astype(r_o.dtype)
    k_o[0] = k2.astype(k_o.dtype)
    v_o[0] = v.astype(v_o.dtype)
    a_o[0] = (-kk).astype(a_o.dtype)
    b_o[0] = (kk * a).astype(b_o.dtype)
    bonus_o[0] = _head_sum(r * k2 * rk_ref[...], ind_ref, indt_ref) * v
    gs_o[0] = _silu(g).astype(gs_o.dtype)


def _pick_heads(res, head_masks):
    c = RWKV_CHUNK
    out = res[0:c]
    for h in range(1, UNIT_HEADS):
        out = jnp.where(head_masks[h], res[h * c:(h + 1) * c], out)
    return out


def _rwkv_intra_kernel(r_ref, lw_ref, k_ref, v_ref, a_ref, b_ref,
                       rh_o, ah_o, bh_o, kh_o, yh_o, vh_o, gc_o,
                       q_ref, apl_ref, tpl_ref, aa_ref, ar_ref, at_ref, rt_ref):
    bsz, tb, uw = r_ref.shape
    c = RWKV_CHUNK
    n_c = tb // c
    nt = (((1,), (1,)), ((), ()))

    ti = lax.broadcasted_iota(jnp.int32, (c, c), 0)
    si = lax.broadcasted_iota(jnp.int32, (c, c), 1)
    tri = (si <= ti).astype(BF16)
    rowt = lax.broadcasted_iota(jnp.int32, (UNIT_HEADS * c, 2 * c), 0) % c
    cols = lax.broadcasted_iota(jnp.int32, (UNIT_HEADS * c, 2 * c), 1) % c
    strict = cols < rowt
    incl = cols <= rowt
    lane_head = lax.broadcasted_iota(jnp.int32, (c, uw), 1) // RWKV_HEAD_DIM
    head_masks = [lane_head == h for h in range(UNIT_HEADS)]
    first_half = lax.broadcasted_iota(jnp.int32, (c, 2 * c), 1) < c
    f32 = lambda ref, b, rows: ref[b, rows, :].astype(F32)

    grp = range(INTRA_GROUP)
    for b in range(bsz):
        def stage_a(gi_, carry, b=b):
            cis = [gi_ * INTRA_GROUP + g for g in grp]
            js = [b * n_c + ci for ci in cis]
            rows = [pl.ds(pl.multiple_of(ci * c, c), c) for ci in cis]
            lws = [lw_ref[b, rows[g], :] for g in grp]
            css = [_cumsum_rows(tri, lws[g]) for g in grp]
            lhss, rhss = [], []
            for g in grp:
                cs = css[g]
                dec = jnp.exp(cs)
                inv = jnp.exp(-cs)
                at = f32(a_ref, b, rows[g]) * jnp.exp(cs - lws[g])
                rt = f32(r_ref, b, rows[g]) * dec
                bt = f32(b_ref, b, rows[g]) * inv
                kt = f32(k_ref, b, rows[g]) * inv
                gc = dec[c - 1:c, :]
                bh_o[b, rows[g], :] = (bt * gc).astype(bh_o.dtype)
                kh_o[b, rows[g], :] = (kt * gc).astype(kh_o.dtype)
                gc_o[b, cis[g]] = jnp.broadcast_to(gc, (SUBLANES, uw))
                at_ref[js[g]] = at.astype(BF16)
                rt_ref[js[g]] = rt
                lhss.append(jnp.concatenate([jnp.where(m, at, 0.0) for m in head_masks]
                                            + [jnp.where(m, rt, 0.0) for m in head_masks],
                                            axis=0).astype(BF16))
                rhss.append(jnp.concatenate([bt, kt], axis=0).astype(BF16))
            prods = [lax.dot_general(lhss[g], rhss[g], nt, preferred_element_type=F32)
                     for g in grp]
            for g in grp:
                da = jnp.where(strict, prods[g][0:UNIT_HEADS * c], 0.0)
                dr = jnp.where(incl, prods[g][UNIT_HEADS * c:], 0.0)
                aa_ref[js[g]] = da.astype(BF16)
                ar_ref[js[g]] = dr.astype(BF16)
                for h in range(UNIT_HEADS):
                    q_ref[pl.ds(UNIT_HEADS * js[g] + h, c, stride=Q_STRIDE), :] = jnp.where(
                        first_half, da[h * c:(h + 1) * c], 0.0)
            return carry
        lax.fori_loop(0, n_c // INTRA_GROUP, stage_a, 0)

    for t in range(c):
        tile = q_ref[t * Q_STRIDE:t * Q_STRIDE + LANES, :]
        apl_ref[t] = tile.T[0:c, :]
    sub = lax.broadcasted_iota(jnp.int32, (SUBLANES, LANES), 0)
    zero8 = jnp.zeros((SUBLANES, LANES), F32)
    for t in range(c):
        nb = t // SUBLANES + 1
        acc = [[None, None] for _ in range(nb)]
        for s in range(t):
            coef = apl_ref[t, s:s + 1, :]
            for jv in range(s // SUBLANES + 1):
                term = coef * tpl_ref[s, jv * SUBLANES:(jv + 1) * SUBLANES, :]
                slot = acc[jv]
                slot[s % 2] = term if slot[s % 2] is None else slot[s % 2] + term
        for jv in range(c // SUBLANES):
            if jv < nb:
                parts = [p for p in acc[jv] if p is not None]
                if jv == nb - 1:
                    parts.append(jnp.where(sub == t % SUBLANES, 1.0, 0.0))
                val = parts[0]
                for p in parts[1:]:
                    val = val + p
            else:
                val = zero8
            tpl_ref[t, jv * SUBLANES:(jv + 1) * SUBLANES, :] = val
    zpad = jnp.zeros((LANES - c, LANES), F32)
    for t in range(c):
        q_ref[t * Q_STRIDE:t * Q_STRIDE + LANES, :] = jnp.concatenate([tpl_ref[t], zpad], axis=0).T

    for b in range(bsz):
        def stage_c(gi_, carry, b=b):
            cis = [gi_ * INTRA_GROUP + g for g in grp]
            js = [b * n_c + ci for ci in cis]
            rows = [pl.ds(pl.multiple_of(ci * c, c), c) for ci in cis]
            zb = jnp.zeros((c, uw), BF16)
            vs = [v_ref[b, rows[g], :] for g in grp]
            akvs = [_pick_heads(jnp.dot(aa_ref[js[g]], jnp.concatenate([zb, vs[g]], axis=0),
                                        preferred_element_type=F32), head_masks)
                    for g in grp]
            tsts = [jnp.concatenate(
                [q_ref[pl.ds(UNIT_HEADS * js[g] + h, c, stride=Q_STRIDE), :][:, 0:c]
                 for h in range(UNIT_HEADS)], axis=0).astype(BF16) for g in grp]
            x4s = [jnp.dot(tsts[g],
                           jnp.concatenate([at_ref[js[g]], akvs[g].astype(BF16)], axis=1),
                           preferred_element_type=F32) for g in grp]
            ahs = [_pick_heads(x4s[g][:, 0:uw], head_masks) for g in grp]
            vhs = [_pick_heads(x4s[g][:, uw:], head_masks) for g in grp]
            y4s = []
            for g in grp:
                top = jnp.concatenate([ahs[g].astype(BF16), vhs[g].astype(BF16)], axis=1)
                bot = jnp.concatenate([zb, vs[g]], axis=1)
                y4s.append(jnp.dot(ar_ref[js[g]], jnp.concatenate([top, bot], axis=0),
                                   preferred_element_type=F32))
            for g in grp:
                rh_o[b, rows[g], :] = (rt_ref[js[g]]
                                       + _pick_heads(y4s[g][:, 0:uw], head_masks)).astype(rh_o.dtype)
                ah_o[b, rows[g], :] = ahs[g].astype(ah_o.dtype)
                yh_o[b, rows[g], :] = _pick_heads(y4s[g][:, uw:], head_masks)
                vh_o[b, rows[g], :] = vhs[g]
            return carry
        lax.fori_loop(0, n_c // INTRA_GROUP, stage_c, 0)


def _rwkv_seq_kernel(rh_ref, ah_ref, bh_ref, kh_ref, v_ref, yh_ref, vh_ref, gc_ref, y_o, s_ref):
    bsz, tb, d = rh_ref.shape
    c = RWKV_CHUNK
    uw = UNIT_W
    n_units = d // uw
    nt = (((1,), (1,)), ((), ()))
    tn = (((0,), (0,)), ((), ()))

    @pl.when(pl.program_id(0) == 0)
    def _():
        s_ref[...] = jnp.zeros(s_ref.shape, F32)

    rid = lax.broadcasted_iota(jnp.int32, (uw, uw), 0) // RWKV_HEAD_DIM
    cid = lax.broadcasted_iota(jnp.int32, (uw, uw), 1) // RWKV_HEAD_DIM
    same_head = rid == cid
    units = [(b, u) for b in range(bsz) for u in range(n_units)]

    for ci in range(tb // c):
        rows = slice(ci * c, (ci + 1) * c)
        prods = []
        for i, (b, u) in enumerate(units):
            ul = slice(u * uw, (u + 1) * uw)
            lhs = jnp.concatenate([rh_ref[b, rows, ul], ah_ref[b, rows, ul]], axis=0)
            prods.append(lax.dot_general(lhs, s_ref[i].astype(BF16), nt,
                                         preferred_element_type=F32))
        us = []
        for i, (b, u) in enumerate(units):
            ul = slice(u * uw, (u + 1) * uw)
            yu = jnp.concatenate([yh_ref[b, rows, ul], vh_ref[b, rows, ul]], axis=0) + prods[i]
            y_o[b, rows, ul] = yu[0:c]
            us.append(yu[c:].astype(BF16))
        for i, (b, u) in enumerate(units):
            ul = slice(u * uw, (u + 1) * uw)
            lhs = jnp.concatenate([us[i], v_ref[b, rows, ul]], axis=0)
            rhs = jnp.concatenate([bh_ref[b, rows, ul], kh_ref[b, rows, ul]], axis=0)
            upd = lax.dot_general(lhs, rhs, tn, preferred_element_type=F32)
            s_ref[i] = s_ref[i] * gc_ref[b, ci, 0:1, ul] + jnp.where(same_head, upd, 0.0)


def _rwkv_post_kernel(y_ref, bonus_ref, gs_ref, gg_ref, gb_ref, ind_ref, indt_ref, o_ref):
    y = y_ref[...]
    inv = 1.0 / RWKV_HEAD_DIM
    mu = _head_sum(y, ind_ref, indt_ref) * inv
    dlt = y - mu
    var = _head_sum(dlt * dlt, ind_ref, indt_ref) * inv
    o = dlt * lax.rsqrt(var + RWKV_GN_EPS) * gg_ref[...] + gb_ref[...]
    o_ref[...] = ((o + bonus_ref[...]) * gs_ref[...].astype(F32)).astype(o_ref.dtype)


def _rwkv_branch(rkvg, small, mu, w0, w_up, a0, a_up, k_k, k_a, r_k, gn_g, gn_b):
    bsz, seq, d4 = rkvg.shape
    d = d4 // 4
    n_heads = d // RWKV_HEAD_DIM
    sw = small.shape[2]
    lo = sw - 2 * RWKV_LORA
    mu = mu.astype(F32)
    mu_main = mu[:4 * d].reshape(1, 4 * d)
    mu_small = jnp.pad(mu[4 * d:], (lo, 0)).reshape(1, sw)
    wup_pad = jnp.pad(w_up.astype(F32), ((lo, RWKV_LORA), (0, 0)))
    aup_pad = jnp.pad(a_up.astype(F32), ((lo + RWKV_LORA, 0), (0, 0)))
    hi_lo = lambda w: (w.astype(BF16), (w - w.astype(BF16).astype(F32)).astype(BF16))
    wup_hi, wup_lo = hi_lo(wup_pad)
    aup_hi, aup_lo = hi_lo(aup_pad)
    head_of = jnp.arange(d) // RWKV_HEAD_DIM
    ind = (head_of[:, None] == jnp.arange(LANES)[None, :]).astype(BF16)
    indt = ind.T
    vec = lambda t: t.reshape(1, d).astype(F32)

    tm = 128
    nt = seq // tm
    hb = tm // SUBLANES
    cur = lambda w: pl.BlockSpec((1, tm, w), lambda b, i: (b, i, 0))
    halo = lambda w: pl.BlockSpec((1, SUBLANES, w), lambda b, i: (b, jnp.maximum(i * hb - 1, 0), 0))
    full = lambda s0, s1: pl.BlockSpec((s0, s1), lambda b, i: (0, 0))
    outs = pl.pallas_call(
        _rwkv_prep_kernel,
        grid=(bsz, nt),
        in_specs=[cur(4 * d), halo(4 * d), cur(sw), halo(sw), full(1, 4 * d), full(1, sw),
                  full(1, d), full(sw, d), full(sw, d), full(1, d), full(sw, d), full(sw, d),
                  full(1, d), full(1, d), full(1, d), full(d, LANES), full(LANES, d)],
        out_specs=[cur(d)] * 8,
        out_shape=[jax.ShapeDtypeStruct((bsz, seq, d), dt)
                   for dt in (BF16, F32, BF16, BF16, BF16, BF16, F32, BF16)],
        compiler_params=_params("parallel", "parallel"),
        name="rwkv_prep",
    )(rkvg, rkvg, small, small, mu_main, mu_small, vec(w0), wup_hi, wup_lo, vec(a0), aup_hi,
      aup_lo, vec(k_k), vec(k_a), vec(r_k), ind, indt)
    r, lw, k, v, na, kb, bonus, gs = outs

    c = RWKV_CHUNK
    uw = UNIT_W
    n_c = LANES // (bsz * UNIT_HEADS)
    tb = n_c * c
    n_prob = bsz * n_c
    blk = pl.BlockSpec((bsz, tb, uw), lambda i, u: (0, i, u))
    shp = lambda dt: jax.ShapeDtypeStruct((bsz, seq, d), dt)
    rh, ah, bh, kh, yh, vh, gc = pl.pallas_call(
        _rwkv_intra_kernel,
        grid=(seq // tb, d // uw),
        in_specs=[blk] * 6,
        out_specs=[blk] * 6 + [pl.BlockSpec((bsz, n_c, SUBLANES, uw), lambda i, u: (0, i, 0, u))],
        out_shape=[shp(BF16), shp(BF16), shp(BF16), shp(BF16), shp(F32), shp(F32),
                   jax.ShapeDtypeStruct((bsz, seq // c, SUBLANES, d), F32)],
        scratch_shapes=[pltpu.VMEM((c * Q_STRIDE, LANES), F32),
                        pltpu.VMEM((c, c, LANES), F32),
                        pltpu.VMEM((c, c, LANES), F32),
                        pltpu.VMEM((n_prob, UNIT_HEADS * c, 2 * c), BF16),
                        pltpu.VMEM((n_prob, UNIT_HEADS * c, 2 * c), BF16),
                        pltpu.VMEM((n_prob, c, uw), BF16),
                        pltpu.VMEM((n_prob, c, uw), F32)],
        compiler_params=_params("parallel", "parallel"),
        name="rwkv_intra",
    )(r, lw, k, v, na, kb)

    ts = 2 * c
    sblk = pl.BlockSpec((bsz, ts, d), lambda i: (0, i, 0))
    y = pl.pallas_call(
        _rwkv_seq_kernel,
        grid=(seq // ts,),
        in_specs=[sblk] * 7 + [pl.BlockSpec((bsz, ts // c, SUBLANES, d), lambda i: (0, i, 0, 0))],
        out_specs=sblk,
        out_shape=shp(F32),
        scratch_shapes=[pltpu.VMEM((bsz * d // uw, uw, uw), F32)],
        compiler_params=_params("arbitrary"),
        name="rwkv_seq",
    )(rh, ah, bh, kh, v, yh, vh, gc)

    m = bsz * seq
    tp = 256
    rows = lambda: pl.BlockSpec((tp, d), lambda i: (i, 0))
    c1 = lambda s0, s1: pl.BlockSpec((s0, s1), lambda i: (0, 0))
    return pl.pallas_call(
        _rwkv_post_kernel,
        grid=(m // tp,),
        in_specs=[rows(), rows(), rows(), c1(1, d), c1(1, d), c1(d, LANES), c1(LANES, d)],
        out_specs=rows(),
        out_shape=jax.ShapeDtypeStruct((m, d), BF16),
        compiler_params=_params("parallel"),
        name="rwkv_post",
    )(y.reshape(m, d), bonus.reshape(m, d), gs.reshape(m, d), vec(gn_g), vec(gn_b), ind, indt)


def _mem_attn_kernel(q_ref, k_ref, v_ref, g_ref, o_ref):
    d = q_ref.shape[2]
    hd = d // MEM_HEADS
    scale = hd ** -0.5
    for h in range(MEM_HEADS):
        sl = slice(h * hd, (h + 1) * hd)
        s = lax.dot_general(q_ref[0, :, sl], k_ref[0, :, sl], (((1,), (1,)), ((), ())),
                            preferred_element_type=F32) * scale
        s = s - jnp.max(s, axis=-1, keepdims=True)
        e = jnp.exp(s)
        p = e / jnp.sum(e, axis=-1, keepdims=True)
        om = jnp.dot(p.astype(BF16), v_ref[0, :, sl], preferred_element_type=F32)
        o_ref[0, :, sl] = (om * _silu(g_ref[0, :, sl])).astype(o_ref.dtype)


def _mem_branch(q, gate, mem_k, mem_v):
    bsz, seq, d = q.shape
    n_mem = mem_k.shape[1]
    tm = 512
    cur = pl.BlockSpec((1, tm, d), lambda b, i: (b, i, 0))
    kv = pl.BlockSpec((1, n_mem, d), lambda b, i: (b, 0, 0))
    return pl.pallas_call(
        _mem_attn_kernel,
        grid=(bsz, seq // tm),
        in_specs=[cur, kv, kv, cur],
        out_specs=cur,
        out_shape=jax.ShapeDtypeStruct((bsz, seq, d), BF16),
        compiler_params=_params("parallel", "parallel"),
        name="mem_attn",
    )(q, mem_k, mem_v, gate)


def _merge_kernel(a1_ref, a2_ref, a3_ref, w1_ref, w2_ref, w3_ref, g1_ref, g2_ref, g3_ref,
                  b1_ref, b2_ref, b3_ref, o_ref):
    def term(a_ref, w_ref, g_ref, b_ref):
        y = jnp.dot(a_ref[...], w_ref[...], preferred_element_type=F32)
        return jax.nn.sigmoid(g_ref[...] + b_ref[...]) * y

    acc = term(a1_ref, w1_ref, g1_ref, b1_ref)
    acc = acc + term(a2_ref, w2_ref, g2_ref, b2_ref)
    acc = acc + term(a3_ref, w3_ref, g3_ref, b3_ref)
    o_ref[...] = acc.astype(o_ref.dtype)


def _merge(a_ssm, a_rwkv, a_mem, w_ssm, w_rwkv, w_mem, gate_raw, gate_b):
    m, d = a_rwkv.shape
    tm, tn = 512, 512
    nb = d // tn
    act = lambda a: pl.BlockSpec((tm, a.shape[1]), lambda j, i: (i, 0))
    wgt = lambda w: pl.BlockSpec((w.shape[0], tn), lambda j, i: (0, j))
    gate = lambda k: pl.BlockSpec((tm, tn), lambda j, i: (i, j + k * nb))
    bias = lambda k: pl.BlockSpec((1, tn), lambda j, i: (0, j + k * nb))
    gb = gate_b.reshape(1, 3 * d).astype(F32)
    return pl.pallas_call(
        _merge_kernel,
        grid=(nb, m // tm),
        in_specs=[act(a_ssm), act(a_rwkv), act(a_mem), wgt(w_ssm), wgt(w_rwkv), wgt(w_mem),
                  gate(0), gate(1), gate(2), bias(0), bias(1), bias(2)],
        out_specs=pl.BlockSpec((tm, tn), lambda j, i: (i, j)),
        out_shape=jax.ShapeDtypeStruct((m, d), BF16),
        compiler_params=_params("parallel", "parallel"),
        name="merge",
    )(a_ssm, a_rwkv, a_mem, w_ssm, w_rwkv, w_mem, gate_raw, gate_raw, gate_raw, gb, gb, gb)


def _out_kernel(m_ref, w_ref, g_ref, x_ref, o_ref):
    y = jnp.dot(m_ref[...], w_ref[...], preferred_element_type=F32)
    ms = jnp.mean(y * y, axis=-1, keepdims=True)
    o_ref[...] = x_ref[...] + y * lax.rsqrt(ms + RMS_EPS) * g_ref[...]


def _out_proj(merged, w_out, g_post, x2d):
    m, d = x2d.shape
    tm = 512
    rows = pl.BlockSpec((tm, d), lambda i: (i, 0))
    return pl.pallas_call(
        _out_kernel,
        grid=(m // tm,),
        in_specs=[rows, pl.BlockSpec((d, d), lambda i: (0, 0)),
                  pl.BlockSpec((1, d), lambda i: (0, 0)), rows],
        out_specs=rows,
        out_shape=jax.ShapeDtypeStruct((m, d), F32),
        compiler_params=_params("parallel"),
        name="out_proj",
    )(merged, w_out, g_post.reshape(1, d).astype(F32), x2d)


def _layer(h, mem, g_pre, w_in, ssm_conv_w, ssm_conv_b, ssm_dt_bias, ssm_a_log, ssm_d, ssm_norm_g,
           rwkv_mu, rwkv_w0, rwkv_w_up, rwkv_a0, rwkv_a_up, rwkv_k_k, rwkv_k_a, rwkv_r_k,
           rwkv_gn_g, rwkv_gn_b, mem_norm_g, mem_w_kv, gate_b, w_br_ssm, w_br_rwkv, w_br_mem,
           w_out, g_post):
    bsz, seq, d = h.shape
    m = bsz * seq
    d_inner = w_br_ssm.shape[0]
    n_ssm_heads = d_inner // SSM_HEAD_DIM
    conv_dim = ssm_conv_w.shape[1]
    d_rwkv = w_br_rwkv.shape[0]
    d_mem = w_br_mem.shape[0]
    x2d = h.reshape(m, d)

    o_z = 0
    o_xbc = o_z + d_inner
    o_dt = o_xbc + conv_dim
    o_rw = o_dt + n_ssm_heads
    o_lora = o_rw + 4 * d_rwkv
    o_q = o_lora + 2 * RWKV_LORA
    o_mg = o_q + d_mem
    o_gate = o_mg + d_mem
    wb = lambda a, b: w_in[:, a:b].astype(BF16)
    w_small = jnp.concatenate([w_in[:, o_dt:o_rw], w_in[:, o_lora:o_q]], axis=1).astype(BF16)

    u = _rmsnorm(x2d, g_pre, BF16, 512)
    zs = _matmul_silu(u, wb(o_z, o_xbc), BF16, 1024, 1024, "proj_z")
    xbc = _matmul(u, wb(o_xbc, o_dt), F32, 1024, 1024, "proj_xbc")
    rkvg = _matmul(u, wb(o_rw, o_lora), F32, 1024, 1024, "proj_rwkv")
    small = _matmul(u, w_small, F32, 1024, w_small.shape[1], "proj_small")
    mem_q = _matmul(u, wb(o_q, o_mg), BF16, 1024, 1024, "proj_memq")
    mem_g = _matmul(u, wb(o_mg, o_gate), F32, 1024, 1024, "proj_memg")
    gate_raw = _matmul(u, wb(o_gate, w_in.shape[1]), F32, 1024, 1024, "proj_gate")

    a_ssm = _ssd_branch(xbc.reshape(bsz, seq, conv_dim), zs.reshape(bsz, seq, d_inner),
                        small.reshape(bsz, seq, -1), ssm_conv_w, ssm_conv_b, ssm_dt_bias,
                        ssm_a_log, ssm_d, ssm_norm_g).reshape(m, d_inner)

    a_rwkv = _rwkv_branch(rkvg.reshape(bsz, seq, 4 * d_rwkv), small.reshape(bsz, seq, -1),
                          rwkv_mu, rwkv_w0, rwkv_w_up, rwkv_a0, rwkv_a_up, rwkv_k_k, rwkv_k_a,
                          rwkv_r_k, rwkv_gn_g, rwkv_gn_b)

    n_mem = mem.shape[1]
    mem_n = _rmsnorm(mem.reshape(bsz * n_mem, d), mem_norm_g, BF16, n_mem)
    mem_kv = _matmul(mem_n, mem_w_kv.astype(BF16), BF16, bsz * n_mem, 1024, "proj_memkv")
    mem_k = mem_kv[:, :d_mem].reshape(bsz, n_mem, d_mem)
    mem_v = mem_kv[:, d_mem:].reshape(bsz, n_mem, d_mem)
    a_mem = _mem_branch(mem_q.reshape(bsz, seq, d_mem), mem_g.reshape(bsz, seq, d_mem),
                        mem_k, mem_v).reshape(m, d_mem)

    merged = _merge(a_ssm, a_rwkv, a_mem, w_br_ssm.astype(BF16), w_br_rwkv.astype(BF16),
                    w_br_mem.astype(BF16), gate_raw, gate_b)
    out = _out_proj(merged, w_out.astype(BF16), g_post, x2d)
    return out.reshape(bsz, seq, d)


def kernel(x, mem, g_pre, w_in, ssm_conv_w, ssm_conv_b, ssm_dt_bias, ssm_a_log, ssm_d, ssm_norm_g, rwkv_mu, rwkv_w0, rwkv_w_up, rwkv_a0, rwkv_a_up, rwkv_k_k, rwkv_k_a, rwkv_r_k, rwkv_gn_g, rwkv_gn_b, mem_norm_g, mem_w_kv, gate_b, w_br_ssm, w_br_rwkv, w_br_mem, w_out, g_post):
    layer_params = (g_pre, w_in, ssm_conv_w, ssm_conv_b, ssm_dt_bias, ssm_a_log, ssm_d, ssm_norm_g,
                    rwkv_mu, rwkv_w0, rwkv_w_up, rwkv_a0, rwkv_a_up, rwkv_k_k, rwkv_k_a, rwkv_r_k,
                    rwkv_gn_g, rwkv_gn_b, mem_norm_g, mem_w_kv, gate_b, w_br_ssm, w_br_rwkv,
                    w_br_mem, w_out, g_post)
    h = x
    for layer in range(g_pre.shape[0]):
        h = _layer(h, mem, *[p[layer] for p in layer_params])
    return h
```

```python
import functools

import jax
import jax.numpy as jnp
from jax import lax
from jax.experimental import pallas as pl
from jax.experimental.pallas import tpu as pltpu

F32 = jnp.float32
BF16 = jnp.bfloat16

RMS_EPS = 1e-6
LOG2E = 1.4426950408889634
SSM_HEAD_DIM = 64
SSM_N_GROUPS = 8
SSM_D_STATE = 128
SSM_CONV_WIDTH = 4
SSM_CHUNK = 128
RWKV_HEAD_DIM = 64
RWKV_LORA = 96
RWKV_GN_EPS = 64e-5
RWKV_CHUNK = 64
UNIT_HEADS = 4
UNIT_W = UNIT_HEADS * RWKV_HEAD_DIM
INTRA_GROUP = 8
MEM_HEADS = 4

LANES = 128
SUBLANES = 8
MXU_N = 256
VMEM_LIMIT = 56 * 1024 * 1024
Q_STRIDE = LANES + SUBLANES


def _params(*sem):
    return pltpu.CompilerParams(dimension_semantics=sem, vmem_limit_bytes=VMEM_LIMIT)


def _silu(x):
    return x * jax.nn.sigmoid(x)


def _softplus(x):
    return jnp.maximum(x, 0.0) + jnp.log(1.0 + jnp.exp(-jnp.abs(x)))


def _split_dot(x, w_bf16):
    hi = x.astype(BF16)
    lo = (x - hi.astype(F32)).astype(BF16)
    return (jnp.dot(hi, w_bf16, preferred_element_type=F32)
            + jnp.dot(lo, w_bf16, preferred_element_type=F32))


def _cumsum_rows(tri_bf16, x):
    hi = x.astype(BF16)
    r1 = x - hi.astype(F32)
    mid = r1.astype(BF16)
    lo = (r1 - mid.astype(F32)).astype(BF16)
    dot = lambda p: jnp.dot(tri_bf16, p, preferred_element_type=F32)
    return dot(hi) + (dot(mid) + dot(lo))


def _dot_hi_lo(x, w_hi_ref, w_lo_ref):
    hi = x.astype(BF16)
    lo = (x - hi.astype(F32)).astype(BF16)
    w_hi = w_hi_ref[...]
    return (jnp.dot(hi, w_hi, preferred_element_type=F32)
            + (jnp.dot(hi, w_lo_ref[...], preferred_element_type=F32)
               + jnp.dot(lo, w_hi, preferred_element_type=F32)))


def _rmsnorm_kernel(x_ref, g_ref, o_ref):
    x = x_ref[...].astype(F32)
    ms = jnp.mean(x * x, axis=-1, keepdims=True)
    o_ref[...] = (x * lax.rsqrt(ms + RMS_EPS) * g_ref[...]).astype(o_ref.dtype)


def _rmsnorm(x2d, g, out_dtype, tm):
    m, d = x2d.shape
    return pl.pallas_call(
        _rmsnorm_kernel,
        grid=(m // tm,),
        in_specs=[pl.BlockSpec((tm, d), lambda i: (i, 0)),
                  pl.BlockSpec((1, d), lambda i: (0, 0))],
        out_specs=pl.BlockSpec((tm, d), lambda i: (i, 0)),
        out_shape=jax.ShapeDtypeStruct((m, d), out_dtype),
        compiler_params=_params("parallel"),
        name="rmsnorm",
    )(x2d, g.reshape(1, d).astype(F32))


def _mm_kernel(a_ref, w_ref, o_ref):
    o_ref[...] = jnp.dot(a_ref[...], w_ref[...],
                         preferred_element_type=F32).astype(o_ref.dtype)


def _matmul(a, w, out_dtype, tm, tn, name):
    m, k = a.shape
    n = w.shape[1]
    tm = min(tm, m)
    tn = min(tn, n)
    return pl.pallas_call(
        _mm_kernel,
        grid=(n // tn, m // tm),
        in_specs=[pl.BlockSpec((tm, k), lambda j, i: (i, 0)),
                  pl.BlockSpec((k, tn), lambda j, i: (0, j))],
        out_specs=pl.BlockSpec((tm, tn), lambda j, i: (i, j)),
        out_shape=jax.ShapeDtypeStruct((m, n), out_dtype),
        compiler_params=_params("parallel", "parallel"),
        name=name,
    )(a, w)


def _mm_silu_kernel(a_ref, w_ref, o_ref):
    a = a_ref[...]
    for n in range(o_ref.shape[1] // MXU_N):
        sl = slice(n * MXU_N, (n + 1) * MXU_N)
        acc = jnp.dot(a, w_ref[:, sl], preferred_element_type=F32)
        o_ref[:, sl] = _silu(acc).astype(o_ref.dtype)


def _matmul_silu(a, w, out_dtype, tm, tn, name):
    m, k = a.shape
    n = w.shape[1]
    return pl.pallas_call(
        _mm_silu_kernel,
        grid=(n // tn, m // tm),
        in_specs=[pl.BlockSpec((tm, k), lambda j, i: (i, 0)),
                  pl.BlockSpec((k, tn), lambda j, i: (0, j))],
        out_specs=pl.BlockSpec((tm, tn), lambda j, i: (i, j)),
        out_shape=jax.ShapeDtypeStruct((m, n), out_dtype),
        compiler_params=_params("parallel", "parallel"),
        name=name,
    )(a, w)


def _ssd_kernel(xbc_ref, zs_ref, sm_ref, cw_ref, cb_ref, dtb_ref, alog_ref, d_ref, ng_ref,
                o_ref, xpad_ref, state_ref, y_ref):
    c = pl.program_id(1)
    d_inner = zs_ref.shape[2]
    d_bc = SSM_N_GROUPS * SSM_D_STATE
    n_pairs = d_inner // LANES
    pairs_per_group = n_pairs // SSM_N_GROUPS
    L = SSM_CHUNK

    @pl.when(c == 0)
    def _():
        xpad_ref[0:SUBLANES, :] = jnp.zeros((SUBLANES, xpad_ref.shape[1]), F32)
        state_ref[...] = jnp.zeros(state_ref.shape, F32)

    xpad_ref[SUBLANES:SUBLANES + L, :] = xbc_ref[0]
    acc = cb_ref[...] + cw_ref[3:4, :] * xpad_ref[SUBLANES:SUBLANES + L, :]
    for kk in range(SSM_CONV_WIDTH - 1):
        shift = SSM_CONV_WIDTH - 1 - kk
        acc = acc + cw_ref[kk:kk + 1, :] * xpad_ref[SUBLANES - shift:SUBLANES - shift + L, :]
    xpad_ref[0:SUBLANES, :] = xpad_ref[L:L + SUBLANES, :]
    act = _silu(acc)

    row = lax.broadcasted_iota(jnp.int32, (L, L), 0)
    col = lax.broadcasted_iota(jnp.int32, (L, L), 1)
    causal = col <= row
    tri = causal.astype(BF16)
    first_head = col < SSM_HEAD_DIM

    dt = _softplus(sm_ref[0][:, :LANES] + dtb_ref[...])
    a_neg = -jnp.exp(alog_ref[...]) * LOG2E
    acs = _cumsum_rows(tri, dt * a_neg)
    acs_t = acs.T
    dt_t = dt.T

    for g in range(SSM_N_GROUPS):
        bg = act[:, d_inner + g * SSM_D_STATE:d_inner + (g + 1) * SSM_D_STATE]
        cg = act[:, d_inner + d_bc + g * SSM_D_STATE:d_inner + d_bc + (g + 1) * SSM_D_STATE]
        bg_b = bg.astype(BF16)
        cg_b = cg.astype(BF16)
        cb = lax.dot_general(cg_b, bg_b, (((1,), (1,)), ((), ())),
                             preferred_element_type=F32)
        bgt_b = bg.T.astype(BF16)
        for q in range(pairs_per_group):
            p = g * pairs_per_group + q
            xp = act[:, p * LANES:(p + 1) * LANES]
            xp_b = xp.astype(BF16)
            ys = []
            cols = []
            dcols = []
            for e in range(2):
                h = 2 * p + e
                col_b = jnp.broadcast_to(acs[:, h:h + 1], (L, L))
                row_b = jnp.broadcast_to(acs_t[h:h + 1, :], (L, L))
                dtrow_b = jnp.broadcast_to(dt_t[h:h + 1, :], (L, L))
                decay = jnp.where(causal, jnp.exp2(col_b - row_b), 0.0)
                mix = (cb * decay * dtrow_b).astype(BF16)
                ys.append(jnp.dot(mix, xp_b, preferred_element_type=F32))
                cols.append(col_b)
                dcols.append(jnp.broadcast_to(dt[:, h:h + 1], (L, L)))
            y_intra = jnp.where(first_head, ys[0], ys[1])
            colsel = jnp.where(first_head, cols[0], cols[1])
            dtsel = jnp.where(first_head, dcols[0], dcols[1])
            st = state_ref[p]
            y_inter = jnp.dot(cg_b, st.astype(BF16), preferred_element_type=F32) * jnp.exp2(colsel)
            y_ref[:, p * LANES:(p + 1) * LANES] = (
                y_intra + y_inter + d_ref[:, p * LANES:(p + 1) * LANES] * xp)
            last = colsel[L - 1:L, :]
            xw = (xp * (jnp.exp2(last - colsel) * dtsel)).astype(BF16)
            state_ref[p] = st * jnp.exp2(last) + jnp.dot(bgt_b, xw, preferred_element_type=F32)

    gw = d_inner // SSM_N_GROUPS
    for g in range(SSM_N_GROUPS):
        sl = slice(g * gw, (g + 1) * gw)
        t = y_ref[:, sl] * zs_ref[0, :, sl].astype(F32)
        ms = jnp.mean(t * t, axis=-1, keepdims=True)
        o_ref[0, :, sl] = (t * lax.rsqrt(ms + RMS_EPS) * ng_ref[:, sl]).astype(o_ref.dtype)


def _ssd_branch(xbc, zs, small, conv_w, conv_b, dt_bias, a_log, d_skip, norm_g):
    bsz, seq, conv_dim = xbc.shape
    d_inner = zs.shape[2]
    n_heads = d_inner // SSM_HEAD_DIM
    nc = seq // SSM_CHUNK
    pad = LANES - n_heads
    dtb = jnp.pad(dt_bias.astype(F32), (0, pad)).reshape(1, LANES)
    alog = jnp.pad(a_log.astype(F32), (0, pad)).reshape(1, LANES)
    d_row = jnp.repeat(d_skip.astype(F32), SSM_HEAD_DIM).reshape(1, d_inner)
    full = lambda b, c: (0, 0)
    return pl.pallas_call(
        _ssd_kernel,
        grid=(bsz, nc),
        in_specs=[pl.BlockSpec((1, SSM_CHUNK, conv_dim), lambda b, c: (b, c, 0)),
                  pl.BlockSpec((1, SSM_CHUNK, d_inner), lambda b, c: (b, c, 0)),
                  pl.BlockSpec((1, SSM_CHUNK, small.shape[2]), lambda b, c: (b, c, 0)),
                  pl.BlockSpec((SSM_CONV_WIDTH, conv_dim), full),
                  pl.BlockSpec((1, conv_dim), full),
                  pl.BlockSpec((1, LANES), full),
                  pl.BlockSpec((1, LANES), full),
                  pl.BlockSpec((1, d_inner), full),
                  pl.BlockSpec((1, d_inner), full)],
        out_specs=pl.BlockSpec((1, SSM_CHUNK, d_inner), lambda b, c: (b, c, 0)),
        out_shape=jax.ShapeDtypeStruct((bsz, seq, d_inner), BF16),
        scratch_shapes=[pltpu.VMEM((SSM_CHUNK + 2 * SUBLANES, conv_dim), F32),
                        pltpu.VMEM((d_inner // LANES, SSM_D_STATE, LANES), F32),
                        pltpu.VMEM((SSM_CHUNK, d_inner), F32)],
        compiler_params=_params("arbitrary", "arbitrary"),
        name="ssd_scan",
    )(xbc, zs, small, conv_w.astype(F32), conv_b.reshape(1, conv_dim).astype(F32),
      dtb, alog, d_row, norm_g.reshape(1, d_inner).astype(F32))


def _head_sum(x, ind_ref, indt_ref):
    s = _split_dot(x, ind_ref[...])
    return _split_dot(s, indt_ref[...])


def _rwkv_prep_kernel(x_ref, xh_ref, s_ref, sh_ref, mu_ref, mus_ref, w0_ref, wup_ref, wupl_ref,
                      a0_ref, aup_ref, aupl_ref, kk_ref, ka_ref, rk_ref, ind_ref, indt_ref,
                      r_o, lw_o, k_o, v_o, a_o, b_o, bonus_o, gs_o):
    i = pl.program_id(1)
    tm = x_ref.shape[1]
    d = r_o.shape[2]
    not_first = (i > 0).astype(F32)

    def shifted(cur, halo):
        prev_row = halo[SUBLANES - 1:SUBLANES, :] * not_first
        rolled = pltpu.roll(cur, 1, 0)
        rid = lax.broadcasted_iota(jnp.int32, cur.shape, 0)
        return jnp.where(rid == 0, prev_row, rolled)

    x = x_ref[0]
    rw = x + (shifted(x, xh_ref[0]) - x) * mu_ref[...]
    s = s_ref[0]
    rs = s + (shifted(s, sh_ref[0]) - s) * mus_ref[...]

    r = rw[:, 0:d]
    k = rw[:, d:2 * d]
    v = rw[:, 2 * d:3 * d]
    g = rw[:, 3 * d:4 * d]

    w_arg = w0_ref[...] + _dot_hi_lo(jnp.tanh(rs), wup_ref, wupl_ref)
    w_log = -_softplus(-w_arg) - 0.5
    lw_o[0] = -jnp.exp(w_log)
    a = jax.nn.sigmoid(a0_ref[...] + _dot_hi_lo(rs, aup_ref, aupl_ref))

    kk = k * kk_ref[...]
    kk = kk * jnp.minimum(lax.rsqrt(_head_sum(kk * kk, ind_ref, indt_ref)), 1e12)
    k2 = k * (1.0 + (a - 1.0) * ka_ref[...])
    r_o[0] = r.astype(r_o.dtype)
    k_o[0] = k2.astype(k_o.dtype)
    v_o[0] = v.astype(v_o.dtype)
    a_o[0] = (-kk).astype(a_o.dtype)
    b_o[0] = (kk * a).astype(b_o.dtype)
    bonus_o[0] = _head_sum(r * k2 * rk_ref[...], ind_ref, indt_ref) * v
    gs_o[0] = _silu(g).astype(gs_o.dtype)


def _rwkv_intra_kernel(r_ref, lw_ref, k_ref, v_ref, a_ref, b_ref,
                       rh_o, ah_o, bh_o, kh_o, yh_o, vh_o, gc_o,
                       q_ref, apl_ref, tpl_ref, aa_ref, ar_ref, at_ref, rt_ref):
    bsz, tb, uw = r_ref.shape
    c = RWKV_CHUNK
    n_c = tb // c
    nt = (((1,), (1,)), ((), ()))

    ti = lax.broadcasted_iota(jnp.int32, (c, c), 0)
    si = lax.broadcasted_iota(jnp.int32, (c, c), 1)
    tri = (si <= ti).astype(BF16)
    pw = 2 * c
    n_pairs = UNIT_HEADS // 2
    prow = lambda q: slice(q * pw, (q + 1) * pw)
    rowt = lax.broadcasted_iota(jnp.int32, (pw, pw), 0) % c
    cols = lax.broadcasted_iota(jnp.int32, (pw, pw), 1) % c
    strict = cols < rowt
    incl = cols <= rowt
    even = lax.broadcasted_iota(jnp.int32, (c, pw), 1) < c
    f32 = lambda ref, b, rows: ref[b, rows, :].astype(F32)

    grp = range(INTRA_GROUP)
    for b in range(bsz):
        def stage_a(gi_, carry, b=b):
            cis = [gi_ * INTRA_GROUP + g for g in grp]
            js = [b * n_c + ci for ci in cis]
            rows = [pl.ds(pl.multiple_of(ci * c, c), c) for ci in cis]
            lws = [lw_ref[b, rows[g], :] for g in grp]
            css = [_cumsum_rows(tri, lws[g]) for g in grp]
            lhss, rhss = [], []
            for g in grp:
                cs = css[g]
                dec = jnp.exp(cs)
                inv = jnp.exp(-cs)
                at = f32(a_ref, b, rows[g]) * jnp.exp(cs - lws[g])
                rt = f32(r_ref, b, rows[g]) * dec
                bt = f32(b_ref, b, rows[g]) * inv
                kt = f32(k_ref, b, rows[g]) * inv
                gc = dec[c - 1:c, :]
                bh_o[b, rows[g], :] = (bt * gc).astype(bh_o.dtype)
                kh_o[b, rows[g], :] = (kt * gc).astype(kh_o.dtype)
                gc_o[b, cis[g]] = jnp.broadcast_to(gc, (SUBLANES, uw))
                at_ref[js[g]] = at.astype(BF16)
                rt_ref[js[g]] = rt
                for q in range(n_pairs):
                    btq, ktq = bt[:, prow(q)], kt[:, prow(q)]
                    lhss.append(jnp.concatenate([at[:, prow(q)], rt[:, prow(q)]],
                                                axis=0).astype(BF16))
                    rhss.append(jnp.concatenate(
                        [jnp.where(even, btq, 0.0), jnp.where(even, ktq, 0.0),
                         jnp.where(even, 0.0, btq), jnp.where(even, 0.0, ktq)],
                        axis=0).astype(BF16))
            prods = [lax.dot_general(lhs, rhs, nt, preferred_element_type=F32)
                     for lhs, rhs in zip(lhss, rhss)]
            for i, prod in enumerate(prods):
                g, q = divmod(i, n_pairs)
                stack = lambda x: jnp.concatenate([x[:, 0:pw], x[:, pw:]], axis=0)
                da = jnp.where(strict, stack(prod[0:c]), 0.0)
                dr = jnp.where(incl, stack(prod[c:]), 0.0)
                aa_ref[js[g], prow(q), :] = da.astype(BF16)
                ar_ref[js[g], prow(q), :] = dr.astype(BF16)
                for e in range(2):
                    q_ref[pl.ds(UNIT_HEADS * js[g] + 2 * q + e, c, stride=Q_STRIDE), :] = (
                        jnp.where(even, da[e * c:(e + 1) * c], 0.0))
            return carry
        lax.fori_loop(0, n_c // INTRA_GROUP, stage_a, 0)

    for t in range(c):
        tile = q_ref[t * Q_STRIDE:t * Q_STRIDE + LANES, :]
        apl_ref[t] = tile.T[0:c, :]
    sub = lax.broadcasted_iota(jnp.int32, (SUBLANES, LANES), 0)
    zero8 = jnp.zeros((SUBLANES, LANES), F32)
    for t in range(c):
        nb = t // SUBLANES + 1
        acc = [[None, None] for _ in range(nb)]
        for s in range(t):
            coef = apl_ref[t, s:s + 1, :]
            for jv in range(s // SUBLANES + 1):
                term = coef * tpl_ref[s, jv * SUBLANES:(jv + 1) * SUBLANES, :]
                slot = acc[jv]
                slot[s % 2] = term if slot[s % 2] is None else slot[s % 2] + term
        for jv in range(c // SUBLANES):
            if jv < nb:
                parts = [p for p in acc[jv] if p is not None]
                if jv == nb - 1:
                    parts.append(jnp.where(sub == t % SUBLANES, 1.0, 0.0))
                val = parts[0]
                for p in parts[1:]:
                    val = val + p
            else:
                val = zero8
            tpl_ref[t, jv * SUBLANES:(jv + 1) * SUBLANES, :] = val
    zpad = jnp.zeros((LANES - c, LANES), F32)
    for t in range(c):
        q_ref[t * Q_STRIDE:t * Q_STRIDE + LANES, :] = jnp.concatenate([tpl_ref[t], zpad], axis=0).T

    for b in range(bsz):
        def stage_c(gi_, carry, b=b):
            cis = [gi_ * INTRA_GROUP + g for g in grp]
            js = [b * n_c + ci for ci in cis]
            rows = [pl.ds(pl.multiple_of(ci * c, c), c) for ci in cis]
            items = [(g, q) for g in grp for q in range(n_pairs)]
            zb = jnp.zeros((c, pw), BF16)

            def pick(res):
                return jnp.where(even, res[0:c], res[c:])

            vs = [v_ref[b, rows[g], prow(q)] for g, q in items]
            akvs = [pick(jnp.dot(aa_ref[js[g], prow(q), :],
                                 jnp.concatenate([zb, vs[i]], axis=0),
                                 preferred_element_type=F32))
                    for i, (g, q) in enumerate(items)]
            tsts = [jnp.concatenate(
                [q_ref[pl.ds(UNIT_HEADS * js[g] + 2 * q + e, c, stride=Q_STRIDE), :][:, 0:c]
                 for e in range(2)], axis=0).astype(BF16) for g, q in items]
            xs = [jnp.dot(tsts[i],
                          jnp.concatenate([at_ref[js[g], :, prow(q)], akvs[i].astype(BF16)], axis=1),
                          preferred_element_type=F32)
                  for i, (g, q) in enumerate(items)]
            ahs = [pick(x[:, 0:pw]) for x in xs]
            vhs = [pick(x[:, pw:]) for x in xs]
            ys = []
            for i, (g, q) in enumerate(items):
                top = jnp.concatenate([ahs[i].astype(BF16), vhs[i].astype(BF16)], axis=1)
                bot = jnp.concatenate([zb, vs[i]], axis=1)
                ys.append(jnp.dot(ar_ref[js[g], prow(q), :], jnp.concatenate([top, bot], axis=0),
                                  preferred_element_type=F32))
            for i, (g, q) in enumerate(items):
                rh_o[b, rows[g], prow(q)] = (rt_ref[js[g], :, prow(q)]
                                             + pick(ys[i][:, 0:pw])).astype(rh_o.dtype)
                ah_o[b, rows[g], prow(q)] = ahs[i].astype(ah_o.dtype)
                yh_o[b, rows[g], prow(q)] = pick(ys[i][:, pw:])
                vh_o[b, rows[g], prow(q)] = vhs[i]
            return carry
        lax.fori_loop(0, n_c // INTRA_GROUP, stage_c, 0)


def _rwkv_seq_kernel(rh_ref, ah_ref, bh_ref, kh_ref, v_ref, yh_ref, vh_ref, gc_ref, y_o, s_ref):
    bsz, tb, d = rh_ref.shape
    c = RWKV_CHUNK
    uw = UNIT_W
    n_units = d // uw
    nt = (((1,), (1,)), ((), ()))
    tn = (((0,), (0,)), ((), ()))

    @pl.when(pl.program_id(0) == 0)
    def _():
        s_ref[...] = jnp.zeros(s_ref.shape, F32)

    rid = lax.broadcasted_iota(jnp.int32, (uw, uw), 0) // RWKV_HEAD_DIM
    cid = lax.broadcasted_iota(jnp.int32, (uw, uw), 1) // RWKV_HEAD_DIM
    same_head = rid == cid
    units = [(b, u) for b in range(bsz) for u in range(n_units)]

    for ci in range(tb // c):
        rows = slice(ci * c, (ci + 1) * c)
        prods = []
        for i, (b, u) in enumerate(units):
            ul = slice(u * uw, (u + 1) * uw)
            lhs = jnp.concatenate([rh_ref[b, rows, ul], ah_ref[b, rows, ul]], axis=0)
            prods.append(lax.dot_general(lhs, s_ref[i].astype(BF16), nt,
                                         preferred_element_type=F32))
        us = []
        for i, (b, u) in enumerate(units):
            ul = slice(u * uw, (u + 1) * uw)
            yu = jnp.concatenate([yh_ref[b, rows, ul], vh_ref[b, rows, ul]], axis=0) + prods[i]
            y_o[b, rows, ul] = yu[0:c]
            us.append(yu[c:].astype(BF16))
        for i, (b, u) in enumerate(units):
            ul = slice(u * uw, (u + 1) * uw)
            lhs = jnp.concatenate([us[i], v_ref[b, rows, ul]], axis=0)
            rhs = jnp.concatenate([bh_ref[b, rows, ul], kh_ref[b, rows, ul]], axis=0)
            upd = lax.dot_general(lhs, rhs, tn, preferred_element_type=F32)
            s_ref[i] = s_ref[i] * gc_ref[b, ci, 0:1, ul] + jnp.where(same_head, upd, 0.0)


def _rwkv_post_kernel(y_ref, bonus_ref, gs_ref, gg_ref, gb_ref, ind_ref, indt_ref, o_ref):
    y = y_ref[...]
    inv = 1.0 / RWKV_HEAD_DIM
    mu = _head_sum(y, ind_ref, indt_ref) * inv
    dlt = y - mu
    var = _head_sum(dlt * dlt, ind_ref, indt_ref) * inv
    o = dlt * lax.rsqrt(var + RWKV_GN_EPS) * gg_ref[...] + gb_ref[...]
    o_ref[...] = ((o + bonus_ref[...]) * gs_ref[...].astype(F32)).astype(o_ref.dtype)


def _rwkv_branch(rkvg, small, mu, w0, w_up, a0, a_up, k_k, k_a, r_k, gn_g, gn_b):
    bsz, seq, d4 = rkvg.shape
    d = d4 // 4
    n_heads = d // RWKV_HEAD_DIM
    sw = small.shape[2]
    lo = sw - 2 * RWKV_LORA
    mu = mu.astype(F32)
    mu_main = mu[:4 * d].reshape(1, 4 * d)
    mu_small = jnp.pad(mu[4 * d:], (lo, 0)).reshape(1, sw)
    wup_pad = jnp.pad(w_up.astype(F32), ((lo, RWKV_LORA), (0, 0)))
    aup_pad = jnp.pad(a_up.astype(F32), ((lo + RWKV_LORA, 0), (0, 0)))
    hi_lo = lambda w: (w.astype(BF16), (w - w.astype(BF16).astype(F32)).astype(BF16))
    wup_hi, wup_lo = hi_lo(wup_pad)
    aup_hi, aup_lo = hi_lo(aup_pad)
    head_of = jnp.arange(d) // RWKV_HEAD_DIM
    ind = (head_of[:, None] == jnp.arange(LANES)[None, :]).astype(BF16)
    indt = ind.T
    vec = lambda t: t.reshape(1, d).astype(F32)

    tm = 128
    nt = seq // tm
    hb = tm // SUBLANES
    cur = lambda w: pl.BlockSpec((1, tm, w), lambda b, i: (b, i, 0))
    halo = lambda w: pl.BlockSpec((1, SUBLANES, w), lambda b, i: (b, jnp.maximum(i * hb - 1, 0), 0))
    full = lambda s0, s1: pl.BlockSpec((s0, s1), lambda b, i: (0, 0))
    outs = pl.pallas_call(
        _rwkv_prep_kernel,
        grid=(bsz, nt),
        in_specs=[cur(4 * d), halo(4 * d), cur(sw), halo(sw), full(1, 4 * d), full(1, sw),
                  full(1, d), full(sw, d), full(sw, d), full(1, d), full(sw, d), full(sw, d),
                  full(1, d), full(1, d), full(1, d), full(d, LANES), full(LANES, d)],
        out_specs=[cur(d)] * 8,
        out_shape=[jax.ShapeDtypeStruct((bsz, seq, d), dt)
                   for dt in (BF16, F32, BF16, BF16, BF16, BF16, F32, BF16)],
        compiler_params=_params("parallel", "parallel"),
        name="rwkv_prep",
    )(rkvg, rkvg, small, small, mu_main, mu_small, vec(w0), wup_hi, wup_lo, vec(a0), aup_hi,
      aup_lo, vec(k_k), vec(k_a), vec(r_k), ind, indt)
    r, lw, k, v, na, kb, bonus, gs = outs

    c = RWKV_CHUNK
    uw = UNIT_W
    n_c = LANES // (bsz * UNIT_HEADS)
    tb = n_c * c
    n_prob = bsz * n_c
    blk = pl.BlockSpec((bsz, tb, uw), lambda i, u: (0, i, u))
    shp = lambda dt: jax.ShapeDtypeStruct((bsz, seq, d), dt)
    rh, ah, bh, kh, yh, vh, gc = pl.pallas_call(
        _rwkv_intra_kernel,
        grid=(seq // tb, d // uw),
        in_specs=[blk] * 6,
        out_specs=[blk] * 6 + [pl.BlockSpec((bsz, n_c, SUBLANES, uw), lambda i, u: (0, i, 0, u))],
        out_shape=[shp(BF16), shp(BF16), shp(BF16), shp(BF16), shp(F32), shp(F32),
                   jax.ShapeDtypeStruct((bsz, seq // c, SUBLANES, d), F32)],
        scratch_shapes=[pltpu.VMEM((c * Q_STRIDE, LANES), F32),
                        pltpu.VMEM((c, c, LANES), F32),
                        pltpu.VMEM((c, c, LANES), F32),
                        pltpu.VMEM((n_prob, UNIT_HEADS * c, 2 * c), BF16),
                        pltpu.VMEM((n_prob, UNIT_HEADS * c, 2 * c), BF16),
                        pltpu.VMEM((n_prob, c, uw), BF16),
                        pltpu.VMEM((n_prob, c, uw), F32)],
        compiler_params=_params("parallel", "parallel"),
        name="rwkv_intra",
    )(r, lw, k, v, na, kb)

    ts = 2 * c
    sblk = pl.BlockSpec((bsz, ts, d), lambda i: (0, i, 0))
    y = pl.pallas_call(
        _rwkv_seq_kernel,
        grid=(seq // ts,),
        in_specs=[sblk] * 7 + [pl.BlockSpec((bsz, ts // c, SUBLANES, d), lambda i: (0, i, 0, 0))],
        out_specs=sblk,
        out_shape=shp(F32),
        scratch_shapes=[pltpu.VMEM((bsz * d // uw, uw, uw), F32)],
        compiler_params=_params("arbitrary"),
        name="rwkv_seq",
    )(rh, ah, bh, kh, v, yh, vh, gc)

    m = bsz * seq
    tp = 256
    rows = lambda: pl.BlockSpec((tp, d), lambda i: (i, 0))
    c1 = lambda s0, s1: pl.BlockSpec((s0, s1), lambda i: (0, 0))
    return pl.pallas_call(
        _rwkv_post_kernel,
        grid=(m // tp,),
        in_specs=[rows(), rows(), rows(), c1(1, d), c1(1, d), c1(d, LANES), c1(LANES, d)],
        out_specs=rows(),
        out_shape=jax.ShapeDtypeStruct((m, d), BF16),
        compiler_params=_params("parallel"),
        name="rwkv_post",
    )(y.reshape(m, d), bonus.reshape(m, d), gs.reshape(m, d), vec(gn_g), vec(gn_b), ind, indt)


def _mem_attn_kernel(q_ref, k_ref, v_ref, g_ref, o_ref):
    d = q_ref.shape[2]
    hd = d // MEM_HEADS
    scale = hd ** -0.5
    for h in range(MEM_HEADS):
        sl = slice(h * hd, (h + 1) * hd)
        s = lax.dot_general(q_ref[0, :, sl], k_ref[0, :, sl], (((1,), (1,)), ((), ())),
                            preferred_element_type=F32) * scale
        s = s - jnp.max(s, axis=-1, keepdims=True)
        e = jnp.exp(s)
        p = e / jnp.sum(e, axis=-1, keepdims=True)
        om = jnp.dot(p.astype(BF16), v_ref[0, :, sl], preferred_element_type=F32)
        o_ref[0, :, sl] = (om * _silu(g_ref[0, :, sl])).astype(o_ref.dtype)


def _mem_branch(q, gate, mem_k, mem_v):
    bsz, seq, d = q.shape
    n_mem = mem_k.shape[1]
    tm = 512
    cur = pl.BlockSpec((1, tm, d), lambda b, i: (b, i, 0))
    kv = pl.BlockSpec((1, n_mem, d), lambda b, i: (b, 0, 0))
    return pl.pallas_call(
        _mem_attn_kernel,
        grid=(bsz, seq // tm),
        in_specs=[cur, kv, kv, cur],
        out_specs=cur,
        out_shape=jax.ShapeDtypeStruct((bsz, seq, d), BF16),
        compiler_params=_params("parallel", "parallel"),
        name="mem_attn",
    )(q, mem_k, mem_v, gate)


def _merge_kernel(a1_ref, a2_ref, a3_ref, w1_ref, w2_ref, w3_ref, g1_ref, g2_ref, g3_ref,
                  b1_ref, b2_ref, b3_ref, o_ref):
    def term(a_ref, w_ref, g_ref, b_ref):
        y = jnp.dot(a_ref[...], w_ref[...], preferred_element_type=F32)
        return jax.nn.sigmoid(g_ref[...] + b_ref[...]) * y

    acc = term(a1_ref, w1_ref, g1_ref, b1_ref)
    acc = acc + term(a2_ref, w2_ref, g2_ref, b2_ref)
    acc = acc + term(a3_ref, w3_ref, g3_ref, b3_ref)
    o_ref[...] = acc.astype(o_ref.dtype)


def _merge(a_ssm, a_rwkv, a_mem, w_ssm, w_rwkv, w_mem, gate_raw, gate_b):
    m, d = a_rwkv.shape
    tm, tn = 512, 512
    nb = d // tn
    act = lambda a: pl.BlockSpec((tm, a.shape[1]), lambda j, i: (i, 0))
    wgt = lambda w: pl.BlockSpec((w.shape[0], tn), lambda j, i: (0, j))
    gate = lambda k: pl.BlockSpec((tm, tn), lambda j, i: (i, j + k * nb))
    bias = lambda k: pl.BlockSpec((1, tn), lambda j, i: (0, j + k * nb))
    gb = gate_b.reshape(1, 3 * d).astype(F32)
    return pl.pallas_call(
        _merge_kernel,
        grid=(nb, m // tm),
        in_specs=[act(a_ssm), act(a_rwkv), act(a_mem), wgt(w_ssm), wgt(w_rwkv), wgt(w_mem),
                  gate(0), gate(1), gate(2), bias(0), bias(1), bias(2)],
        out_specs=pl.BlockSpec((tm, tn), lambda j, i: (i, j)),
        out_shape=jax.ShapeDtypeStruct((m, d), BF16),
        compiler_params=_params("parallel", "parallel"),
        name="merge",
    )(a_ssm, a_rwkv, a_mem, w_ssm, w_rwkv, w_mem, gate_raw, gate_raw, gate_raw, gb, gb, gb)


def _out_kernel(m_ref, w_ref, g_ref, x_ref, o_ref):
    y = jnp.dot(m_ref[...], w_ref[...], preferred_element_type=F32)
    ms = jnp.mean(y * y, axis=-1, keepdims=True)
    o_ref[...] = x_ref[...] + y * lax.rsqrt(ms + RMS_EPS) * g_ref[...]


def _out_proj(merged, w_out, g_post, x2d):
    m, d = x2d.shape
    tm = 512
    rows = pl.BlockSpec((tm, d), lambda i: (i, 0))
    return pl.pallas_call(
        _out_kernel,
        grid=(m // tm,),
        in_specs=[rows, pl.BlockSpec((d, d), lambda i: (0, 0)),
                  pl.BlockSpec((1, d), lambda i: (0, 0)), rows],
        out_specs=rows,
        out_shape=jax.ShapeDtypeStruct((m, d), F32),
        compiler_params=_params("parallel"),
        name="out_proj",
    )(merged, w_out, g_post.reshape(1, d).astype(F32), x2d)


def _layer(h, mem, g_pre, w_in, ssm_conv_w, ssm_conv_b, ssm_dt_bias, ssm_a_log, ssm_d, ssm_norm_g,
           rwkv_mu, rwkv_w0, rwkv_w_up, rwkv_a0, rwkv_a_up, rwkv_k_k, rwkv_k_a, rwkv_r_k,
           rwkv_gn_g, rwkv_gn_b, mem_norm_g, mem_w_kv, gate_b, w_br_ssm, w_br_rwkv, w_br_mem,
           w_out, g_post):
    bsz, seq, d = h.shape
    m = bsz * seq
    d_inner = w_br_ssm.shape[0]
    n_ssm_heads = d_inner // SSM_HEAD_DIM
    conv_dim = ssm_conv_w.shape[1]
    d_rwkv = w_br_rwkv.shape[0]
    d_mem = w_br_mem.shape[0]
    x2d = h.reshape(m, d)

    o_z = 0
    o_xbc = o_z + d_inner
    o_dt = o_xbc + conv_dim
    o_rw = o_dt + n_ssm_heads
    o_lora = o_rw + 4 * d_rwkv
    o_q = o_lora + 2 * RWKV_LORA
    o_mg = o_q + d_mem
    o_gate = o_mg + d_mem
    wb = lambda a, b: w_in[:, a:b].astype(BF16)
    w_small = jnp.concatenate([w_in[:, o_dt:o_rw], w_in[:, o_lora:o_q]], axis=1).astype(BF16)

    u = _rmsnorm(x2d, g_pre, BF16, 512)
    zs = _matmul_silu(u, wb(o_z, o_xbc), BF16, 1024, 1024, "proj_z")
    xbc = _matmul(u, wb(o_xbc, o_dt), F32, 1024, 1024, "proj_xbc")
    rkvg = _matmul(u, wb(o_rw, o_lora), F32, 2048, 1024, "proj_rwkv")
    small = _matmul(u, w_small, F32, 1024, w_small.shape[1], "proj_small")
    mem_q = _matmul(u, wb(o_q, o_mg), BF16, 1024, 1024, "proj_memq")
    mem_g = _matmul(u, wb(o_mg, o_gate), F32, 1024, 1024, "proj_memg")
    gate_raw = _matmul(u, wb(o_gate, w_in.shape[1]), F32, 1024, 1024, "proj_gate")

    a_ssm = _ssd_branch(xbc.reshape(bsz, seq, conv_dim), zs.reshape(bsz, seq, d_inner),
                        small.reshape(bsz, seq, -1), ssm_conv_w, ssm_conv_b, ssm_dt_bias,
                        ssm_a_log, ssm_d, ssm_norm_g).reshape(m, d_inner)

    a_rwkv = _rwkv_branch(rkvg.reshape(bsz, seq, 4 * d_rwkv), small.reshape(bsz, seq, -1),
                          rwkv_mu, rwkv_w0, rwkv_w_up, rwkv_a0, rwkv_a_up, rwkv_k_k, rwkv_k_a,
                          rwkv_r_k, rwkv_gn_g, rwkv_gn_b)

    n_mem = mem.shape[1]
    mem_n = _rmsnorm(mem.reshape(bsz * n_mem, d), mem_norm_g, BF16, n_mem)
    mem_kv = _matmul(mem_n, mem_w_kv.astype(BF16), BF16, bsz * n_mem, 1024, "proj_memkv")
    mem_k = mem_kv[:, :d_mem].reshape(bsz, n_mem, d_mem)
    mem_v = mem_kv[:, d_mem:].reshape(bsz, n_mem, d_mem)
    a_mem = _mem_branch(mem_q.reshape(bsz, seq, d_mem), mem_g.reshape(bsz, seq, d_mem),
                        mem_k, mem_v).reshape(m, d_mem)

    merged = _merge(a_ssm, a_rwkv, a_mem, w_br_ssm.astype(BF16), w_br_rwkv.astype(BF16),
                    w_br_mem.astype(BF16), gate_raw, gate_b)
    out = _out_proj(merged, w_out.astype(BF16), g_post, x2d)
    return out.reshape(bsz, seq, d)


def kernel(x, mem, g_pre, w_in, ssm_conv_w, ssm_conv_b, ssm_dt_bias, ssm_a_log, ssm_d, ssm_norm_g, rwkv_mu, rwkv_w0, rwkv_w_up, rwkv_a0, rwkv_a_up, rwkv_k_k, rwkv_k_a, rwkv_r_k, rwkv_gn_g, rwkv_gn_b, mem_norm_g, mem_w_kv, gate_b, w_br_ssm, w_br_rwkv, w_br_mem, w_out, g_post):
    layer_params = (g_pre, w_in, ssm_conv_w, ssm_conv_b, ssm_dt_bias, ssm_a_log, ssm_d, ssm_norm_g,
                    rwkv_mu, rwkv_w0, rwkv_w_up, rwkv_a0, rwkv_a_up, rwkv_k_k, rwkv_k_a, rwkv_r_k,
                    rwkv_gn_g, rwkv_gn_b, mem_norm_g, mem_w_kv, gate_b, w_br_ssm, w_br_rwkv,
                    w_br_mem, w_out, g_post)
    h = x
    for layer in range(g_pre.shape[0]):
        h = _layer(h, mem, *[p[layer] for p in layer_params])
    return h
```

```python
import functools

import jax
import jax.numpy as jnp
from jax import lax
from jax.experimental import pallas as pl
from jax.experimental.pallas import tpu as pltpu

F32 = jnp.float32
BF16 = jnp.bfloat16

RMS_EPS = 1e-6
LOG2E = 1.4426950408889634
SSM_HEAD_DIM = 64
SSM_N_GROUPS = 8
SSM_D_STATE = 128
SSM_CONV_WIDTH = 4
SSM_CHUNK = 128
RWKV_HEAD_DIM = 64
RWKV_LORA = 96
RWKV_GN_EPS = 64e-5
RWKV_CHUNK = 64
UNIT_HEADS = 4
UNIT_W = UNIT_HEADS * RWKV_HEAD_DIM
INTRA_GROUP = 8
MEM_HEADS = 4

LANES = 128
SUBLANES = 8
MXU_N = 256
VMEM_LIMIT = 56 * 1024 * 1024
Q_STRIDE = LANES + SUBLANES


def _params(*sem):
    return pltpu.CompilerParams(dimension_semantics=sem, vmem_limit_bytes=VMEM_LIMIT)


def _silu(x):
    return x * jax.nn.sigmoid(x)


def _softplus(x):
    return jnp.maximum(x, 0.0) + jnp.log(1.0 + jnp.exp(-jnp.abs(x)))


def _split_dot(x, w_bf16):
    hi = x.astype(BF16)
    lo = (x - hi.astype(F32)).astype(BF16)
    return (jnp.dot(hi, w_bf16, preferred_element_type=F32)
            + jnp.dot(lo, w_bf16, preferred_element_type=F32))


def _cumsum_rows(tri_bf16, x):
    hi = x.astype(BF16)
    r1 = x - hi.astype(F32)
    mid = r1.astype(BF16)
    lo = (r1 - mid.astype(F32)).astype(BF16)
    dot = lambda p: jnp.dot(tri_bf16, p, preferred_element_type=F32)
    return dot(hi) + (dot(mid) + dot(lo))


def _dot_hi_lo(x, w_hi_ref, w_lo_ref):
    hi = x.astype(BF16)
    lo = (x - hi.astype(F32)).astype(BF16)
    w_hi = w_hi_ref[...]
    return (jnp.dot(hi, w_hi, preferred_element_type=F32)
            + (jnp.dot(hi, w_lo_ref[...], preferred_element_type=F32)
               + jnp.dot(lo, w_hi, preferred_element_type=F32)))


def _rmsnorm_kernel(x_ref, g_ref, o_ref):
    x = x_ref[...].astype(F32)
    ms = jnp.mean(x * x, axis=-1, keepdims=True)
    o_ref[...] = (x * lax.rsqrt(ms + RMS_EPS) * g_ref[...]).astype(o_ref.dtype)


def _rmsnorm(x2d, g, out_dtype, tm):
    m, d = x2d.shape
    return pl.pallas_call(
        _rmsnorm_kernel,
        grid=(m // tm,),
        in_specs=[pl.BlockSpec((tm, d), lambda i: (i, 0)),
                  pl.BlockSpec((1, d), lambda i: (0, 0))],
        out_specs=pl.BlockSpec((tm, d), lambda i: (i, 0)),
        out_shape=jax.ShapeDtypeStruct((m, d), out_dtype),
        compiler_params=_params("parallel"),
        name="rmsnorm",
    )(x2d, g.reshape(1, d).astype(F32))


def _mm_kernel(a_ref, w_ref, o_ref):
    o_ref[...] = jnp.dot(a_ref[...], w_ref[...],
                         preferred_element_type=F32).astype(o_ref.dtype)


def _matmul(a, w, out_dtype, tm, tn, name):
    m, k = a.shape
    n = w.shape[1]
    tm = min(tm, m)
    tn = min(tn, n)
    return pl.pallas_call(
        _mm_kernel,
        grid=(n // tn, m // tm),
        in_specs=[pl.BlockSpec((tm, k), lambda j, i: (i, 0)),
                  pl.BlockSpec((k, tn), lambda j, i: (0, j))],
        out_specs=pl.BlockSpec((tm, tn), lambda j, i: (i, j)),
        out_shape=jax.ShapeDtypeStruct((m, n), out_dtype),
        compiler_params=_params("parallel", "parallel"),
        name=name,
    )(a, w)


def _mm_silu_kernel(a_ref, w_ref, o_ref):
    a = a_ref[...]
    for n in range(o_ref.shape[1] // MXU_N):
        sl = slice(n * MXU_N, (n + 1) * MXU_N)
        acc = jnp.dot(a, w_ref[:, sl], preferred_element_type=F32)
        o_ref[:, sl] = _silu(acc).astype(o_ref.dtype)


def _matmul_silu(a, w, out_dtype, tm, tn, name):
    m, k = a.shape
    n = w.shape[1]
    return pl.pallas_call(
        _mm_silu_kernel,
        grid=(n // tn, m // tm),
        in_specs=[pl.BlockSpec((tm, k), lambda j, i: (i, 0)),
                  pl.BlockSpec((k, tn), lambda j, i: (0, j))],
        out_specs=pl.BlockSpec((tm, tn), lambda j, i: (i, j)),
        out_shape=jax.ShapeDtypeStruct((m, n), out_dtype),
        compiler_params=_params("parallel", "parallel"),
        name=name,
    )(a, w)


def _ssd_kernel(xbc_ref, zs_ref, sm_ref, cw_ref, cb_ref, dtb_ref, alog_ref, d_ref, ng_ref,
                u_ref, wx_ref, o_ref, px_ref, xpad_ref, state_ref, y_ref):
    c = pl.program_id(1)
    d_inner = zs_ref.shape[2]
    d_bc = SSM_N_GROUPS * SSM_D_STATE
    n_pairs = d_inner // LANES
    pairs_per_group = n_pairs // SSM_N_GROUPS
    L = SSM_CHUNK

    @pl.when(c == 0)
    def _():
        xpad_ref[0:SUBLANES, :] = jnp.zeros((SUBLANES, xpad_ref.shape[1]), F32)
        state_ref[...] = jnp.zeros(state_ref.shape, F32)

    xpad_ref[SUBLANES:SUBLANES + L, :] = xbc_ref[0]
    acc = cb_ref[...] + cw_ref[3:4, :] * xpad_ref[SUBLANES:SUBLANES + L, :]
    for kk in range(SSM_CONV_WIDTH - 1):
        shift = SSM_CONV_WIDTH - 1 - kk
        acc = acc + cw_ref[kk:kk + 1, :] * xpad_ref[SUBLANES - shift:SUBLANES - shift + L, :]
    xpad_ref[0:SUBLANES, :] = xpad_ref[L:L + SUBLANES, :]
    act = _silu(acc)

    row = lax.broadcasted_iota(jnp.int32, (L, L), 0)
    col = lax.broadcasted_iota(jnp.int32, (L, L), 1)
    causal = col <= row
    tri = causal.astype(BF16)
    first_head = col < SSM_HEAD_DIM

    dt = _softplus(sm_ref[0][:, :LANES] + dtb_ref[...])
    a_neg = -jnp.exp(alog_ref[...]) * LOG2E
    acs = _cumsum_rows(tri, dt * a_neg)
    acs_t = acs.T
    dt_t = dt.T

    px_w = px_ref.shape[2] // SSM_N_GROUPS
    u_rows = u_ref[0]

    for g in range(SSM_N_GROUPS):
        px_ref[0, :, g * px_w:(g + 1) * px_w] = jnp.dot(
            u_rows, wx_ref[:, g * px_w:(g + 1) * px_w],
            preferred_element_type=F32).astype(px_ref.dtype)
        bg = act[:, d_inner + g * SSM_D_STATE:d_inner + (g + 1) * SSM_D_STATE]
        cg = act[:, d_inner + d_bc + g * SSM_D_STATE:d_inner + d_bc + (g + 1) * SSM_D_STATE]
        bg_b = bg.astype(BF16)
        cg_b = cg.astype(BF16)
        cb = lax.dot_general(cg_b, bg_b, (((1,), (1,)), ((), ())),
                             preferred_element_type=F32)
        bgt_b = bg.T.astype(BF16)
        for q in range(pairs_per_group):
            p = g * pairs_per_group + q
            xp = act[:, p * LANES:(p + 1) * LANES]
            xp_b = xp.astype(BF16)
            ys = []
            cols = []
            dcols = []
            for e in range(2):
                h = 2 * p + e
                col_b = jnp.broadcast_to(acs[:, h:h + 1], (L, L))
                row_b = jnp.broadcast_to(acs_t[h:h + 1, :], (L, L))
                dtrow_b = jnp.broadcast_to(dt_t[h:h + 1, :], (L, L))
                decay = jnp.where(causal, jnp.exp2(col_b - row_b), 0.0)
                mix = (cb * decay * dtrow_b).astype(BF16)
                ys.append(jnp.dot(mix, xp_b, preferred_element_type=F32))
                cols.append(col_b)
                dcols.append(jnp.broadcast_to(dt[:, h:h + 1], (L, L)))
            y_intra = jnp.where(first_head, ys[0], ys[1])
            colsel = jnp.where(first_head, cols[0], cols[1])
            dtsel = jnp.where(first_head, dcols[0], dcols[1])
            st = state_ref[p]
            y_inter = jnp.dot(cg_b, st.astype(BF16), preferred_element_type=F32) * jnp.exp2(colsel)
            y_ref[:, p * LANES:(p + 1) * LANES] = (
                y_intra + y_inter + d_ref[:, p * LANES:(p + 1) * LANES] * xp)
            last = colsel[L - 1:L, :]
            xw = (xp * (jnp.exp2(last - colsel) * dtsel)).astype(BF16)
            state_ref[p] = st * jnp.exp2(last) + jnp.dot(bgt_b, xw, preferred_element_type=F32)

    gw = d_inner // SSM_N_GROUPS
    for g in range(SSM_N_GROUPS):
        sl = slice(g * gw, (g + 1) * gw)
        t = y_ref[:, sl] * zs_ref[0, :, sl].astype(F32)
        ms = jnp.mean(t * t, axis=-1, keepdims=True)
        o_ref[0, :, sl] = (t * lax.rsqrt(ms + RMS_EPS) * ng_ref[:, sl]).astype(o_ref.dtype)


def _ssd_branch(xbc, zs, small, conv_w, conv_b, dt_bias, a_log, d_skip, norm_g, u, w_extra):
    bsz, seq, conv_dim = xbc.shape
    d_model, d_extra = w_extra.shape
    d_inner = zs.shape[2]
    n_heads = d_inner // SSM_HEAD_DIM
    nc = seq // SSM_CHUNK
    pad = LANES - n_heads
    dtb = jnp.pad(dt_bias.astype(F32), (0, pad)).reshape(1, LANES)
    alog = jnp.pad(a_log.astype(F32), (0, pad)).reshape(1, LANES)
    d_row = jnp.repeat(d_skip.astype(F32), SSM_HEAD_DIM).reshape(1, d_inner)
    full = lambda b, c: (0, 0)
    return pl.pallas_call(
        _ssd_kernel,
        grid=(bsz, nc),
        in_specs=[pl.BlockSpec((1, SSM_CHUNK, conv_dim), lambda b, c: (b, c, 0)),
                  pl.BlockSpec((1, SSM_CHUNK, d_inner), lambda b, c: (b, c, 0)),
                  pl.BlockSpec((1, SSM_CHUNK, small.shape[2]), lambda b, c: (b, c, 0)),
                  pl.BlockSpec((SSM_CONV_WIDTH, conv_dim), full),
                  pl.BlockSpec((1, conv_dim), full),
                  pl.BlockSpec((1, LANES), full),
                  pl.BlockSpec((1, LANES), full),
                  pl.BlockSpec((1, d_inner), full),
                  pl.BlockSpec((1, d_inner), full),
                  pl.BlockSpec((1, SSM_CHUNK, d_model), lambda b, c: (b, c, 0)),
                  pl.BlockSpec((d_model, d_extra), full)],
        out_specs=[pl.BlockSpec((1, SSM_CHUNK, d_inner), lambda b, c: (b, c, 0)),
                   pl.BlockSpec((1, SSM_CHUNK, d_extra), lambda b, c: (b, c, 0))],
        out_shape=[jax.ShapeDtypeStruct((bsz, seq, d_inner), BF16),
                   jax.ShapeDtypeStruct((bsz, seq, d_extra), F32)],
        scratch_shapes=[pltpu.VMEM((SSM_CHUNK + 2 * SUBLANES, conv_dim), F32),
                        pltpu.VMEM((d_inner // LANES, SSM_D_STATE, LANES), F32),
                        pltpu.VMEM((SSM_CHUNK, d_inner), F32)],
        compiler_params=_params("arbitrary", "arbitrary"),
        name="ssd_scan",
    )(xbc, zs, small, conv_w.astype(F32), conv_b.reshape(1, conv_dim).astype(F32),
      dtb, alog, d_row, norm_g.reshape(1, d_inner).astype(F32),
      u.reshape(bsz, seq, d_model), w_extra)


def _head_sum(x, ind_ref, indt_ref):
    s = _split_dot(x, ind_ref[...])
    return _split_dot(s, indt_ref[...])


def _rwkv_prep_kernel(x_ref, xh_ref, s_ref, sh_ref, mu_ref, mus_ref, w0_ref, wup_ref, wupl_ref,
                      a0_ref, aup_ref, aupl_ref, kk_ref, ka_ref, rk_ref, ind_ref, indt_ref,
                      r_o, lw_o, k_o, v_o, a_o, b_o, bonus_o, gs_o):
    i = pl.program_id(1)
    tm = x_ref.shape[1]
    d = r_o.shape[2]
    not_first = (i > 0).astype(F32)

    def shifted(cur, halo):
        prev_row = halo[SUBLANES - 1:SUBLANES, :] * not_first
        rolled = pltpu.roll(cur, 1, 0)
        rid = lax.broadcasted_iota(jnp.int32, cur.shape, 0)
        return jnp.where(rid == 0, prev_row, rolled)

    x = x_ref[0]
    rw = x + (shifted(x, xh_ref[0]) - x) * mu_ref[...]
    s = s_ref[0]
    rs = s + (shifted(s, sh_ref[0]) - s) * mus_ref[...]

    r = rw[:, 0:d]
    k = rw[:, d:2 * d]
    v = rw[:, 2 * d:3 * d]
    g = rw[:, 3 * d:4 * d]

    w_arg = w0_ref[...] + _dot_hi_lo(jnp.tanh(rs), wup_ref, wupl_ref)
    w_log = -_softplus(-w_arg) - 0.5
    lw_o[0] = -jnp.exp(w_log)
    a = jax.nn.sigmoid(a0_ref[...] + _dot_hi_lo(rs, aup_ref, aupl_ref))

    kk = k * kk_ref[...]
    kk = kk * jnp.minimum(lax.rsqrt(_head_sum(kk * kk, ind_ref, indt_ref)), 1e12)
    k2 = k * (1.0 + (a - 1.0) * ka_ref[...])
    r_o[0] = r.astype(r_o.dtype)
    k_o[0] = k2.astype(k_o.dtype)
    v_o[0] = v.astype(v_o.dtype)
    a_o[0] = (-kk).astype(a_o.dtype)
    b_o[0] = (kk * a).astype(b_o.dtype)
    bonus_o[0] = _head_sum(r * k2 * rk_ref[...], ind_ref, indt_ref) * v
    gs_o[0] = _silu(g).astype(gs_o.dtype)


def _rwkv_intra_kernel(r_ref, lw_ref, k_ref, v_ref, a_ref, b_ref,
                       rh_o, ah_o, bh_o, kh_o, yh_o, vh_o, gc_o,
                       q_ref, apl_ref, tpl_ref, aa_ref, ar_ref, at_ref, rt_ref):
    bsz, tb, uw = r_ref.shape
    c = RWKV_CHUNK
    n_c = tb // c
    nt = (((1,), (1,)), ((), ()))

    ti = lax.broadcasted_iota(jnp.int32, (c, c), 0)
    si = lax.broadcasted_iota(jnp.int32, (c, c), 1)
    tri = (si <= ti).astype(BF16)
    pw = 2 * c
    n_pairs = UNIT_HEADS // 2
    prow = lambda q: slice(q * pw, (q + 1) * pw)
    rowt = lax.broadcasted_iota(jnp.int32, (pw, pw), 0) % c
    cols = lax.broadcasted_iota(jnp.int32, (pw, pw), 1) % c
    strict = cols < rowt
    incl = cols <= rowt
    even = lax.broadcasted_iota(jnp.int32, (c, pw), 1) < c
    f32 = lambda ref, b, rows: ref[b, rows, :].astype(F32)

    grp = range(INTRA_GROUP)
    for b in range(bsz):
        def stage_a(gi_, carry, b=b):
            cis = [gi_ * INTRA_GROUP + g for g in grp]
            js = [b * n_c + ci for ci in cis]
            rows = [pl.ds(pl.multiple_of(ci * c, c), c) for ci in cis]
            lws = [lw_ref[b, rows[g], :] for g in grp]
            css = [_cumsum_rows(tri, lws[g]) for g in grp]
            lhss, rhss = [], []
            for g in grp:
                cs = css[g]
                dec = jnp.exp(cs)
                inv = jnp.exp(-cs)
                at = f32(a_ref, b, rows[g]) * jnp.exp(cs - lws[g])
                rt = f32(r_ref, b, rows[g]) * dec
                bt = f32(b_ref, b, rows[g]) * inv
                kt = f32(k_ref, b, rows[g]) * inv
                gc = dec[c - 1:c, :]
                bh_o[b, rows[g], :] = (bt * gc).astype(bh_o.dtype)
                kh_o[b, rows[g], :] = (kt * gc).astype(kh_o.dtype)
                gc_o[b, cis[g]] = jnp.broadcast_to(gc, (SUBLANES, uw))
                at_ref[js[g]] = at.astype(BF16)
                rt_ref[js[g]] = rt
                for q in range(n_pairs):
                    btq, ktq = bt[:, prow(q)], kt[:, prow(q)]
                    lhss.append(jnp.concatenate([at[:, prow(q)], rt[:, prow(q)]],
                                                axis=0).astype(BF16))
                    rhss.append(jnp.concatenate(
                        [jnp.where(even, btq, 0.0), jnp.where(even, ktq, 0.0),
                         jnp.where(even, 0.0, btq), jnp.where(even, 0.0, ktq)],
                        axis=0).astype(BF16))
            prods = [lax.dot_general(lhs, rhs, nt, preferred_element_type=F32)
                     for lhs, rhs in zip(lhss, rhss)]
            for i, prod in enumerate(prods):
                g, q = divmod(i, n_pairs)
                stack = lambda x: jnp.concatenate([x[:, 0:pw], x[:, pw:]], axis=0)
                da = jnp.where(strict, stack(prod[0:c]), 0.0)
                dr = jnp.where(incl, stack(prod[c:]), 0.0)
                aa_ref[js[g], prow(q), :] = da.astype(BF16)
                ar_ref[js[g], prow(q), :] = dr.astype(BF16)
                for e in range(2):
                    q_ref[pl.ds(UNIT_HEADS * js[g] + 2 * q + e, c, stride=Q_STRIDE), :] = (
                        jnp.where(even, da[e * c:(e + 1) * c], 0.0))
            return carry
        lax.fori_loop(0, n_c // INTRA_GROUP, stage_a, 0)

    for t in range(c):
        tile = q_ref[t * Q_STRIDE:t * Q_STRIDE + LANES, :]
        apl_ref[t] = tile.T[0:c, :]
    sub = lax.broadcasted_iota(jnp.int32, (SUBLANES, LANES), 0)
    zero8 = jnp.zeros((SUBLANES, LANES), F32)
    for t in range(c):
        nb = t // SUBLANES + 1
        acc = [[None, None] for _ in range(nb)]
        for s in range(t):
            coef = apl_ref[t, s:s + 1, :]
            for jv in range(s // SUBLANES + 1):
                term = coef * tpl_ref[s, jv * SUBLANES:(jv + 1) * SUBLANES, :]
                slot = acc[jv]
                slot[s % 2] = term if slot[s % 2] is None else slot[s % 2] + term
        for jv in range(c // SUBLANES):
            if jv < nb:
                parts = [p for p in acc[jv] if p is not None]
                if jv == nb - 1:
                    parts.append(jnp.where(sub == t % SUBLANES, 1.0, 0.0))
                val = parts[0]
                for p in parts[1:]:
                    val = val + p
            else:
                val = zero8
            tpl_ref[t, jv * SUBLANES:(jv + 1) * SUBLANES, :] = val
    zpad = jnp.zeros((LANES - c, LANES), F32)
    for t in range(c):
        q_ref[t * Q_STRIDE:t * Q_STRIDE + LANES, :] = jnp.concatenate([tpl_ref[t], zpad], axis=0).T

    for b in range(bsz):
        def stage_c(gi_, carry, b=b):
            cis = [gi_ * INTRA_GROUP + g for g in grp]
            js = [b * n_c + ci for ci in cis]
            rows = [pl.ds(pl.multiple_of(ci * c, c), c) for ci in cis]
            items = [(g, q) for g in grp for q in range(n_pairs)]
            zb = jnp.zeros((c, pw), BF16)

            def pick(res):
                return jnp.where(even, res[0:c], res[c:])

            vs = [v_ref[b, rows[g], prow(q)] for g, q in items]
            akvs = [pick(jnp.dot(aa_ref[js[g], prow(q), :],
                                 jnp.concatenate([zb, vs[i]], axis=0),
                                 preferred_element_type=F32))
                    for i, (g, q) in enumerate(items)]
            tsts = [jnp.concatenate(
                [q_ref[pl.ds(UNIT_HEADS * js[g] + 2 * q + e, c, stride=Q_STRIDE), :][:, 0:c]
                 for e in range(2)], axis=0).astype(BF16) for g, q in items]
            xs = [jnp.dot(tsts[i],
                          jnp.concatenate([at_ref[js[g], :, prow(q)], akvs[i].astype(BF16)], axis=1),
                          preferred_element_type=F32)
                  for i, (g, q) in enumerate(items)]
            ahs = [pick(x[:, 0:pw]) for x in xs]
            vhs = [pick(x[:, pw:]) for x in xs]
            ys = []
            for i, (g, q) in enumerate(items):
                top = jnp.concatenate([ahs[i].astype(BF16), vhs[i].astype(BF16)], axis=1)
                bot = jnp.concatenate([zb, vs[i]], axis=1)
                ys.append(jnp.dot(ar_ref[js[g], prow(q), :], jnp.concatenate([top, bot], axis=0),
                                  preferred_element_type=F32))
            for i, (g, q) in enumerate(items):
                rh_o[b, rows[g], prow(q)] = (rt_ref[js[g], :, prow(q)]
                                             + pick(ys[i][:, 0:pw])).astype(rh_o.dtype)
                ah_o[b, rows[g], prow(q)] = ahs[i].astype(ah_o.dtype)
                yh_o[b, rows[g], prow(q)] = pick(ys[i][:, pw:])
                vh_o[b, rows[g], prow(q)] = vhs[i]
            return carry
        lax.fori_loop(0, n_c // INTRA_GROUP, stage_c, 0)


def _rwkv_seq_kernel(rh_ref, ah_ref, bh_ref, kh_ref, v_ref, yh_ref, vh_ref, gc_ref, y_o, s_ref):
    bsz, tb, d = rh_ref.shape
    c = RWKV_CHUNK
    uw = UNIT_W
    n_units = d // uw
    nt = (((1,), (1,)), ((), ()))
    tn = (((0,), (0,)), ((), ()))

    @pl.when(pl.program_id(0) == 0)
    def _():
        s_ref[...] = jnp.zeros(s_ref.shape, F32)

    rid = lax.broadcasted_iota(jnp.int32, (uw, uw), 0) // RWKV_HEAD_DIM
    cid = lax.broadcasted_iota(jnp.int32, (uw, uw), 1) // RWKV_HEAD_DIM
    same_head = rid == cid
    units = [(b, u) for b in range(bsz) for u in range(n_units)]

    for ci in range(tb // c):
        rows = slice(ci * c, (ci + 1) * c)
        prods = []
        for i, (b, u) in enumerate(units):
            ul = slice(u * uw, (u + 1) * uw)
            lhs = jnp.concatenate([rh_ref[b, rows, ul], ah_ref[b, rows, ul]], axis=0)
            prods.append(lax.dot_general(lhs, s_ref[i].astype(BF16), nt,
                                         preferred_element_type=F32))
        us = []
        for i, (b, u) in enumerate(units):
            ul = slice(u * uw, (u + 1) * uw)
            yu = jnp.concatenate([yh_ref[b, rows, ul], vh_ref[b, rows, ul]], axis=0) + prods[i]
            y_o[b, rows, ul] = yu[0:c]
            us.append(yu[c:].astype(BF16))
        for i, (b, u) in enumerate(units):
            ul = slice(u * uw, (u + 1) * uw)
            lhs = jnp.concatenate([us[i], v_ref[b, rows, ul]], axis=0)
            rhs = jnp.concatenate([bh_ref[b, rows, ul], kh_ref[b, rows, ul]], axis=0)
            upd = lax.dot_general(lhs, rhs, tn, preferred_element_type=F32)
            s_ref[i] = s_ref[i] * gc_ref[b, ci, 0:1, ul] + jnp.where(same_head, upd, 0.0)


def _rwkv_post_kernel(y_ref, bonus_ref, gs_ref, gg_ref, gb_ref, ind_ref, indt_ref, o_ref):
    y = y_ref[...]
    inv = 1.0 / RWKV_HEAD_DIM
    mu = _head_sum(y, ind_ref, indt_ref) * inv
    dlt = y - mu
    var = _head_sum(dlt * dlt, ind_ref, indt_ref) * inv
    o = dlt * lax.rsqrt(var + RWKV_GN_EPS) * gg_ref[...] + gb_ref[...]
    o_ref[...] = ((o + bonus_ref[...]) * gs_ref[...].astype(F32)).astype(o_ref.dtype)


def _rwkv_branch(rkvg, small, mu, w0, w_up, a0, a_up, k_k, k_a, r_k, gn_g, gn_b):
    bsz, seq, d4 = rkvg.shape
    d = d4 // 4
    n_heads = d // RWKV_HEAD_DIM
    sw = small.shape[2]
    lo = sw - 2 * RWKV_LORA
    mu = mu.astype(F32)
    mu_main = mu[:4 * d].reshape(1, 4 * d)
    mu_small = jnp.pad(mu[4 * d:], (lo, 0)).reshape(1, sw)
    wup_pad = jnp.pad(w_up.astype(F32), ((lo, RWKV_LORA), (0, 0)))
    aup_pad = jnp.pad(a_up.astype(F32), ((lo + RWKV_LORA, 0), (0, 0)))
    hi_lo = lambda w: (w.astype(BF16), (w - w.astype(BF16).astype(F32)).astype(BF16))
    wup_hi, wup_lo = hi_lo(wup_pad)
    aup_hi, aup_lo = hi_lo(aup_pad)
    head_of = jnp.arange(d) // RWKV_HEAD_DIM
    ind = (head_of[:, None] == jnp.arange(LANES)[None, :]).astype(BF16)
    indt = ind.T
    vec = lambda t: t.reshape(1, d).astype(F32)

    tm = 128
    nt = seq // tm
    hb = tm // SUBLANES
    cur = lambda w: pl.BlockSpec((1, tm, w), lambda b, i: (b, i, 0))
    halo = lambda w: pl.BlockSpec((1, SUBLANES, w), lambda b, i: (b, jnp.maximum(i * hb - 1, 0), 0))
    full = lambda s0, s1: pl.BlockSpec((s0, s1), lambda b, i: (0, 0))
    outs = pl.pallas_call(
        _rwkv_prep_kernel,
        grid=(bsz, nt),
        in_specs=[cur(4 * d), halo(4 * d), cur(sw), halo(sw), full(1, 4 * d), full(1, sw),
                  full(1, d), full(sw, d), full(sw, d), full(1, d), full(sw, d), full(sw, d),
                  full(1, d), full(1, d), full(1, d), full(d, LANES), full(LANES, d)],
        out_specs=[cur(d)] * 8,
        out_shape=[jax.ShapeDtypeStruct((bsz, seq, d), dt)
                   for dt in (BF16, F32, BF16, BF16, BF16, BF16, F32, BF16)],
        compiler_params=_params("parallel", "parallel"),
        name="rwkv_prep",
    )(rkvg, rkvg, small, small, mu_main, mu_small, vec(w0), wup_hi, wup_lo, vec(a0), aup_hi,
      aup_lo, vec(k_k), vec(k_a), vec(r_k), ind, indt)
    r, lw, k, v, na, kb, bonus, gs = outs

    c = RWKV_CHUNK
    uw = UNIT_W
    n_c = LANES // (bsz * UNIT_HEADS)
    tb = n_c * c
    n_prob = bsz * n_c
    blk = pl.BlockSpec((bsz, tb, uw), lambda i, u: (0, i, u))
    shp = lambda dt: jax.ShapeDtypeStruct((bsz, seq, d), dt)
    rh, ah, bh, kh, yh, vh, gc = pl.pallas_call(
        _rwkv_intra_kernel,
        grid=(seq // tb, d // uw),
        in_specs=[blk] * 6,
        out_specs=[blk] * 6 + [pl.BlockSpec((bsz, n_c, SUBLANES, uw), lambda i, u: (0, i, 0, u))],
        out_shape=[shp(BF16), shp(BF16), shp(BF16), shp(BF16), shp(F32), shp(F32),
                   jax.ShapeDtypeStruct((bsz, seq // c, SUBLANES, d), F32)],
        scratch_shapes=[pltpu.VMEM((c * Q_STRIDE, LANES), F32),
                        pltpu.VMEM((c, c, LANES), F32),
                        pltpu.VMEM((c, c, LANES), F32),
                        pltpu.VMEM((n_prob, UNIT_HEADS * c, 2 * c), BF16),
                        pltpu.VMEM((n_prob, UNIT_HEADS * c, 2 * c), BF16),
                        pltpu.VMEM((n_prob, c, uw), BF16),
                        pltpu.VMEM((n_prob, c, uw), F32)],
        compiler_params=_params("parallel", "parallel"),
        name="rwkv_intra",
    )(r, lw, k, v, na, kb)

    ts = 2 * c
    sblk = pl.BlockSpec((bsz, ts, d), lambda i: (0, i, 0))
    y = pl.pallas_call(
        _rwkv_seq_kernel,
        grid=(seq // ts,),
        in_specs=[sblk] * 7 + [pl.BlockSpec((bsz, ts // c, SUBLANES, d), lambda i: (0, i, 0, 0))],
        out_specs=sblk,
        out_shape=shp(F32),
        scratch_shapes=[pltpu.VMEM((bsz * d // uw, uw, uw), F32)],
        compiler_params=_params("arbitrary"),
        name="rwkv_seq",
    )(rh, ah, bh, kh, v, yh, vh, gc)

    m = bsz * seq
    tp = 256
    rows = lambda: pl.BlockSpec((tp, d), lambda i: (i, 0))
    c1 = lambda s0, s1: pl.BlockSpec((s0, s1), lambda i: (0, 0))
    return pl.pallas_call(
        _rwkv_post_kernel,
        grid=(m // tp,),
        in_specs=[rows(), rows(), rows(), c1(1, d), c1(1, d), c1(d, LANES), c1(LANES, d)],
        out_specs=rows(),
        out_shape=jax.ShapeDtypeStruct((m, d), BF16),
        compiler_params=_params("parallel"),
        name="rwkv_post",
    )(y.reshape(m, d), bonus.reshape(m, d), gs.reshape(m, d), vec(gn_g), vec(gn_b), ind, indt)


def _mem_attn_kernel(q_ref, k_ref, v_ref, g_ref, o_ref):
    d = q_ref.shape[2]
    hd = d // MEM_HEADS
    scale = hd ** -0.5
    for h in range(MEM_HEADS):
        sl = slice(h * hd, (h + 1) * hd)
        s = lax.dot_general(q_ref[0, :, sl], k_ref[0, :, sl], (((1,), (1,)), ((), ())),
                            preferred_element_type=F32) * scale
        s = s - jnp.max(s, axis=-1, keepdims=True)
        e = jnp.exp(s)
        p = e / jnp.sum(e, axis=-1, keepdims=True)
        om = jnp.dot(p.astype(BF16), v_ref[0, :, sl], preferred_element_type=F32)
        o_ref[0, :, sl] = (om * _silu(g_ref[0, :, sl])).astype(o_ref.dtype)


def _mem_branch(q, gate, mem_k, mem_v):
    bsz, seq, d = q.shape
    n_mem = mem_k.shape[1]
    tm = 512
    cur = pl.BlockSpec((1, tm, d), lambda b, i: (b, i, 0))
    kv = pl.BlockSpec((1, n_mem, d), lambda b, i: (b, 0, 0))
    return pl.pallas_call(
        _mem_attn_kernel,
        grid=(bsz, seq // tm),
        in_specs=[cur, kv, kv, cur],
        out_specs=cur,
        out_shape=jax.ShapeDtypeStruct((bsz, seq, d), BF16),
        compiler_params=_params("parallel", "parallel"),
        name="mem_attn",
    )(q, mem_k, mem_v, gate)


def _merge_kernel(a1_ref, a2_ref, a3_ref, w1_ref, w2_ref, w3_ref, g1_ref, g2_ref, g3_ref,
                  b1_ref, b2_ref, b3_ref, o_ref):
    def term(a_ref, w_ref, g_ref, b_ref):
        y = jnp.dot(a_ref[...], w_ref[...], preferred_element_type=F32)
        return jax.nn.sigmoid(g_ref[...] + b_ref[...]) * y

    acc = term(a1_ref, w1_ref, g1_ref, b1_ref)
    acc = acc + term(a2_ref, w2_ref, g2_ref, b2_ref)
    acc = acc + term(a3_ref, w3_ref, g3_ref, b3_ref)
    o_ref[...] = acc.astype(o_ref.dtype)


def _merge(a_ssm, a_rwkv, a_mem, w_ssm, w_rwkv, w_mem, gate_raw, gate_b):
    m, d = a_rwkv.shape
    tm, tn = 512, 512
    nb = d // tn
    act = lambda a: pl.BlockSpec((tm, a.shape[1]), lambda j, i: (i, 0))
    wgt = lambda w: pl.BlockSpec((w.shape[0], tn), lambda j, i: (0, j))
    gate = lambda k: pl.BlockSpec((tm, tn), lambda j, i: (i, j + k * nb))
    bias = lambda k: pl.BlockSpec((1, tn), lambda j, i: (0, j + k * nb))
    gb = gate_b.reshape(1, 3 * d).astype(F32)
    return pl.pallas_call(
        _merge_kernel,
        grid=(nb, m // tm),
        in_specs=[act(a_ssm), act(a_rwkv), act(a_mem), wgt(w_ssm), wgt(w_rwkv), wgt(w_mem),
                  gate(0), gate(1), gate(2), bias(0), bias(1), bias(2)],
        out_specs=pl.BlockSpec((tm, tn), lambda j, i: (i, j)),
        out_shape=jax.ShapeDtypeStruct((m, d), BF16),
        compiler_params=_params("parallel", "parallel"),
        name="merge",
    )(a_ssm, a_rwkv, a_mem, w_ssm, w_rwkv, w_mem, gate_raw, gate_raw, gate_raw, gb, gb, gb)


def _out_kernel(m_ref, w_ref, g_ref, x_ref, o_ref):
    y = jnp.dot(m_ref[...], w_ref[...], preferred_element_type=F32)
    ms = jnp.mean(y * y, axis=-1, keepdims=True)
    o_ref[...] = x_ref[...] + y * lax.rsqrt(ms + RMS_EPS) * g_ref[...]


def _out_proj(merged, w_out, g_post, x2d):
    m, d = x2d.shape
    tm = 512
    rows = pl.BlockSpec((tm, d), lambda i: (i, 0))
    return pl.pallas_call(
        _out_kernel,
        grid=(m // tm,),
        in_specs=[rows, pl.BlockSpec((d, d), lambda i: (0, 0)),
                  pl.BlockSpec((1, d), lambda i: (0, 0)), rows],
        out_specs=rows,
        out_shape=jax.ShapeDtypeStruct((m, d), F32),
        compiler_params=_params("parallel"),
        name="out_proj",
    )(merged, w_out, g_post.reshape(1, d).astype(F32), x2d)


def _layer(h, mem, g_pre, w_in, ssm_conv_w, ssm_conv_b, ssm_dt_bias, ssm_a_log, ssm_d, ssm_norm_g,
           rwkv_mu, rwkv_w0, rwkv_w_up, rwkv_a0, rwkv_a_up, rwkv_k_k, rwkv_k_a, rwkv_r_k,
           rwkv_gn_g, rwkv_gn_b, mem_norm_g, mem_w_kv, gate_b, w_br_ssm, w_br_rwkv, w_br_mem,
           w_out, g_post):
    bsz, seq, d = h.shape
    m = bsz * seq
    d_inner = w_br_ssm.shape[0]
    n_ssm_heads = d_inner // SSM_HEAD_DIM
    conv_dim = ssm_conv_w.shape[1]
    d_rwkv = w_br_rwkv.shape[0]
    d_mem = w_br_mem.shape[0]
    x2d = h.reshape(m, d)

    o_z = 0
    o_xbc = o_z + d_inner
    o_dt = o_xbc + conv_dim
    o_rw = o_dt + n_ssm_heads
    o_lora = o_rw + 4 * d_rwkv
    o_q = o_lora + 2 * RWKV_LORA
    o_mg = o_q + d_mem
    o_gate = o_mg + d_mem
    wb = lambda a, b: w_in[:, a:b].astype(BF16)
    w_small = jnp.concatenate([w_in[:, o_dt:o_rw], w_in[:, o_lora:o_q]], axis=1).astype(BF16)

    u = _rmsnorm(x2d, g_pre, BF16, 512)
    zs = _matmul_silu(u, wb(o_z, o_xbc), BF16, 1024, 1024, "proj_z")
    xbc = _matmul(u, wb(o_xbc, o_dt), F32, 2048, 1024, "proj_xbc")
    rkvg = _matmul(u, wb(o_rw, o_lora), F32, 2048, 1024, "proj_rwkv")
    small = _matmul(u, w_small, F32, 1024, w_small.shape[1], "proj_small")
    mem_q = _matmul(u, wb(o_q, o_mg), BF16, 2048, 1024, "proj_memq")
    gate_raw = _matmul(u, wb(o_gate, w_in.shape[1]), F32, 2048, 1024, "proj_gate")

    a_ssm, mem_g = _ssd_branch(xbc.reshape(bsz, seq, conv_dim), zs.reshape(bsz, seq, d_inner),
                               small.reshape(bsz, seq, -1), ssm_conv_w, ssm_conv_b, ssm_dt_bias,
                               ssm_a_log, ssm_d, ssm_norm_g, u, wb(o_mg, o_gate))
    a_ssm = a_ssm.reshape(m, d_inner)

    a_rwkv = _rwkv_branch(rkvg.reshape(bsz, seq, 4 * d_rwkv), small.reshape(bsz, seq, -1),
                          rwkv_mu, rwkv_w0, rwkv_w_up, rwkv_a0, rwkv_a_up, rwkv_k_k, rwkv_k_a,
                          rwkv_r_k, rwkv_gn_g, rwkv_gn_b)

    n_mem = mem.shape[1]
    mem_n = _rmsnorm(mem.reshape(bsz * n_mem, d), mem_norm_g, BF16, n_mem)
    mem_kv = _matmul(mem_n, mem_w_kv.astype(BF16), BF16, bsz * n_mem, 1024, "proj_memkv")
    mem_k = mem_kv[:, :d_mem].reshape(bsz, n_mem, d_mem)
    mem_v = mem_kv[:, d_mem:].reshape(bsz, n_mem, d_mem)
    a_mem = _mem_branch(mem_q.reshape(bsz, seq, d_mem), mem_g.reshape(bsz, seq, d_mem),
                        mem_k, mem_v).reshape(m, d_mem)

    merged = _merge(a_ssm, a_rwkv, a_mem, w_br_ssm.astype(BF16), w_br_rwkv.astype(BF16),
                    w_br_mem.astype(BF16), gate_raw, gate_b)
    out = _out_proj(merged, w_out.astype(BF16), g_post, x2d)
    return out.reshape(bsz, seq, d)


def kernel(x, mem, g_pre, w_in, ssm_conv_w, ssm_conv_b, ssm_dt_bias, ssm_a_log, ssm_d, ssm_norm_g, rwkv_mu, rwkv_w0, rwkv_w_up, rwkv_a0, rwkv_a_up, rwkv_k_k, rwkv_k_a, rwkv_r_k, rwkv_gn_g, rwkv_gn_b, mem_norm_g, mem_w_kv, gate_b, w_br_ssm, w_br_rwkv, w_br_mem, w_out, g_post):
    layer_params = (g_pre, w_in, ssm_conv_w, ssm_conv_b, ssm_dt_bias, ssm_a_log, ssm_d, ssm_norm_g,
                    rwkv_mu, rwkv_w0, rwkv_w_up, rwkv_a0, rwkv_a_up, rwkv_k_k, rwkv_k_a, rwkv_r_k,
                    rwkv_gn_g, rwkv_gn_b, mem_norm_g, mem_w_kv, gate_b, w_br_ssm, w_br_rwkv,
                    w_br_mem, w_out, g_post)
    h = x
    for layer in range(g_pre.shape[0]):
        h = _layer(h, mem, *[p[layer] for p in layer_params])
    return h
```

```python
import functools

import jax
import jax.numpy as jnp
from jax import lax
from jax.experimental import pallas as pl
from jax.experimental.pallas import tpu as pltpu

F32 = jnp.float32
BF16 = jnp.bfloat16

RMS_EPS = 1e-6
LOG2E = 1.4426950408889634
SSM_HEAD_DIM = 64
SSM_N_GROUPS = 8
SSM_D_STATE = 128
SSM_CONV_WIDTH = 4
SSM_CHUNK = 128
RWKV_HEAD_DIM = 64
RWKV_LORA = 96
RWKV_GN_EPS = 64e-5
RWKV_CHUNK = 64
UNIT_HEADS = 4
UNIT_W = UNIT_HEADS * RWKV_HEAD_DIM
INTRA_GROUP = 8
MEM_HEADS = 4

LANES = 128
SUBLANES = 8
MXU_N = 256
VMEM_LIMIT = 56 * 1024 * 1024
Q_STRIDE = LANES + SUBLANES


def _params(*sem):
    return pltpu.CompilerParams(dimension_semantics=sem, vmem_limit_bytes=VMEM_LIMIT)


def _silu(x):
    return x * jax.nn.sigmoid(x)


def _softplus(x):
    return jnp.maximum(x, 0.0) + jnp.log(1.0 + jnp.exp(-jnp.abs(x)))


def _split_dot(x, w_bf16):
    hi = x.astype(BF16)
    lo = (x - hi.astype(F32)).astype(BF16)
    return (jnp.dot(hi, w_bf16, preferred_element_type=F32)
            + jnp.dot(lo, w_bf16, preferred_element_type=F32))


def _cumsum_rows(tri_bf16, x):
    hi = x.astype(BF16)
    r1 = x - hi.astype(F32)
    mid = r1.astype(BF16)
    lo = (r1 - mid.astype(F32)).astype(BF16)
    dot = lambda p: jnp.dot(tri_bf16, p, preferred_element_type=F32)
    return dot(hi) + (dot(mid) + dot(lo))


def _dot_hi_lo(x, w_hi_ref, w_lo_ref):
    hi = x.astype(BF16)
    lo = (x - hi.astype(F32)).astype(BF16)
    w_hi = w_hi_ref[...]
    return (jnp.dot(hi, w_hi, preferred_element_type=F32)
            + (jnp.dot(hi, w_lo_ref[...], preferred_element_type=F32)
               + jnp.dot(lo, w_hi, preferred_element_type=F32)))


def _rmsnorm_kernel(x_ref, g_ref, o_ref):
    x = x_ref[...].astype(F32)
    ms = jnp.mean(x * x, axis=-1, keepdims=True)
    o_ref[...] = (x * lax.rsqrt(ms + RMS_EPS) * g_ref[...]).astype(o_ref.dtype)


def _rmsnorm(x2d, g, out_dtype, tm):
    m, d = x2d.shape
    return pl.pallas_call(
        _rmsnorm_kernel,
        grid=(m // tm,),
        in_specs=[pl.BlockSpec((tm, d), lambda i: (i, 0)),
                  pl.BlockSpec((1, d), lambda i: (0, 0))],
        out_specs=pl.BlockSpec((tm, d), lambda i: (i, 0)),
        out_shape=jax.ShapeDtypeStruct((m, d), out_dtype),
        compiler_params=_params("parallel"),
        name="rmsnorm",
    )(x2d, g.reshape(1, d).astype(F32))


def _mm_kernel(a_ref, w_ref, o_ref):
    o_ref[...] = jnp.dot(a_ref[...], w_ref[...],
                         preferred_element_type=F32).astype(o_ref.dtype)


def _matmul(a, w, out_dtype, tm, tn, name):
    m, k = a.shape
    n = w.shape[1]
    tm = min(tm, m)
    tn = min(tn, n)
    return pl.pallas_call(
        _mm_kernel,
        grid=(n // tn, m // tm),
        in_specs=[pl.BlockSpec((tm, k), lambda j, i: (i, 0)),
                  pl.BlockSpec((k, tn), lambda j, i: (0, j))],
        out_specs=pl.BlockSpec((tm, tn), lambda j, i: (i, j)),
        out_shape=jax.ShapeDtypeStruct((m, n), out_dtype),
        compiler_params=_params("parallel", "parallel"),
        name=name,
    )(a, w)


def _mm_silu_kernel(a_ref, w_ref, o_ref):
    a = a_ref[...]
    for n in range(o_ref.shape[1] // MXU_N):
        sl = slice(n * MXU_N, (n + 1) * MXU_N)
        acc = jnp.dot(a, w_ref[:, sl], preferred_element_type=F32)
        o_ref[:, sl] = _silu(acc).astype(o_ref.dtype)


def _matmul_silu(a, w, out_dtype, tm, tn, name):
    m, k = a.shape
    n = w.shape[1]
    return pl.pallas_call(
        _mm_silu_kernel,
        grid=(n // tn, m // tm),
        in_specs=[pl.BlockSpec((tm, k), lambda j, i: (i, 0)),
                  pl.BlockSpec((k, tn), lambda j, i: (0, j))],
        out_specs=pl.BlockSpec((tm, tn), lambda j, i: (i, j)),
        out_shape=jax.ShapeDtypeStruct((m, n), out_dtype),
        compiler_params=_params("parallel", "parallel"),
        name=name,
    )(a, w)


def _ssd_kernel(xbc_ref, zs_ref, sm_ref, cw_ref, cb_ref, dtb_ref, alog_ref, d_ref, ng_ref,
                u_ref, wx_ref, o_ref, px_ref, xpad_ref, state_ref, y_ref):
    c = pl.program_id(1)
    d_inner = zs_ref.shape[2]
    d_bc = SSM_N_GROUPS * SSM_D_STATE
    n_pairs = d_inner // LANES
    pairs_per_group = n_pairs // SSM_N_GROUPS
    L = SSM_CHUNK

    @pl.when(c == 0)
    def _():
        xpad_ref[0:SUBLANES, :] = jnp.zeros((SUBLANES, xpad_ref.shape[1]), F32)
        state_ref[...] = jnp.zeros(state_ref.shape, F32)

    xpad_ref[SUBLANES:SUBLANES + L, :] = xbc_ref[0]
    acc = cb_ref[...] + cw_ref[3:4, :] * xpad_ref[SUBLANES:SUBLANES + L, :]
    for kk in range(SSM_CONV_WIDTH - 1):
        shift = SSM_CONV_WIDTH - 1 - kk
        acc = acc + cw_ref[kk:kk + 1, :] * xpad_ref[SUBLANES - shift:SUBLANES - shift + L, :]
    xpad_ref[0:SUBLANES, :] = xpad_ref[L:L + SUBLANES, :]
    act = _silu(acc)

    row = lax.broadcasted_iota(jnp.int32, (L, L), 0)
    col = lax.broadcasted_iota(jnp.int32, (L, L), 1)
    causal = col <= row
    tri = causal.astype(BF16)
    first_head = col < SSM_HEAD_DIM

    dt = _softplus(sm_ref[0][:, :LANES] + dtb_ref[...])
    a_neg = -jnp.exp(alog_ref[...]) * LOG2E
    acs = _cumsum_rows(tri, dt * a_neg)
    acs_t = acs.T
    dt_t = dt.T

    px_w = px_ref.shape[2] // SSM_N_GROUPS
    u_rows = u_ref[0]

    for g in range(SSM_N_GROUPS):
        px_ref[0, :, g * px_w:(g + 1) * px_w] = jnp.dot(
            u_rows, wx_ref[:, g * px_w:(g + 1) * px_w],
            preferred_element_type=F32).astype(px_ref.dtype)
        bg = act[:, d_inner + g * SSM_D_STATE:d_inner + (g + 1) * SSM_D_STATE]
        cg = act[:, d_inner + d_bc + g * SSM_D_STATE:d_inner + d_bc + (g + 1) * SSM_D_STATE]
        bg_b = bg.astype(BF16)
        cg_b = cg.astype(BF16)
        cb = lax.dot_general(cg_b, bg_b, (((1,), (1,)), ((), ())),
                             preferred_element_type=F32)
        bgt_b = bg.T.astype(BF16)
        for q in range(pairs_per_group):
            p = g * pairs_per_group + q
            xp = act[:, p * LANES:(p + 1) * LANES]
            xp_b = xp.astype(BF16)
            ys = []
            cols = []
            dcols = []
            for e in range(2):
                h = 2 * p + e
                col_b = jnp.broadcast_to(acs[:, h:h + 1], (L, L))
                row_b = jnp.broadcast_to(acs_t[h:h + 1, :], (L, L))
                dtrow_b = jnp.broadcast_to(dt_t[h:h + 1, :], (L, L))
                decay = jnp.where(causal, jnp.exp2(col_b - row_b), 0.0)
                mix = (cb * decay * dtrow_b).astype(BF16)
                ys.append(jnp.dot(mix, xp_b, preferred_element_type=F32))
                cols.append(col_b)
                dcols.append(jnp.broadcast_to(dt[:, h:h + 1], (L, L)))
            y_intra = jnp.where(first_head, ys[0], ys[1])
            colsel = jnp.where(first_head, cols[0], cols[1])
            dtsel = jnp.where(first_head, dcols[0], dcols[1])
            st = state_ref[p]
            y_inter = jnp.dot(cg_b, st.astype(BF16), preferred_element_type=F32) * jnp.exp2(colsel)
            y_ref[:, p * LANES:(p + 1) * LANES] = (
                y_intra + y_inter + d_ref[:, p * LANES:(p + 1) * LANES] * xp)
            last = colsel[L - 1:L, :]
            xw = (xp * (jnp.exp2(last - colsel) * dtsel)).astype(BF16)
            state_ref[p] = st * jnp.exp2(last) + jnp.dot(bgt_b, xw, preferred_element_type=F32)

    gw = d_inner // SSM_N_GROUPS
    for g in range(SSM_N_GROUPS):
        sl = slice(g * gw, (g + 1) * gw)
        t = y_ref[:, sl] * zs_ref[0, :, sl].astype(F32)
        ms = jnp.mean(t * t, axis=-1, keepdims=True)
        o_ref[0, :, sl] = (t * lax.rsqrt(ms + RMS_EPS) * ng_ref[:, sl]).astype(o_ref.dtype)


def _ssd_branch(xbc, zs, small, conv_w, conv_b, dt_bias, a_log, d_skip, norm_g, u, w_extra):
    bsz, seq, conv_dim = xbc.shape
    d_model, d_extra = w_extra.shape
    d_inner = zs.shape[2]
    n_heads = d_inner // SSM_HEAD_DIM
    nc = seq // SSM_CHUNK
    pad = LANES - n_heads
    dtb = jnp.pad(dt_bias.astype(F32), (0, pad)).reshape(1, LANES)
    alog = jnp.pad(a_log.astype(F32), (0, pad)).reshape(1, LANES)
    d_row = jnp.repeat(d_skip.astype(F32), SSM_HEAD_DIM).reshape(1, d_inner)
    full = lambda b, c: (0, 0)
    return pl.pallas_call(
        _ssd_kernel,
        grid=(bsz, nc),
        in_specs=[pl.BlockSpec((1, SSM_CHUNK, conv_dim), lambda b, c: (b, c, 0)),
                  pl.BlockSpec((1, SSM_CHUNK, d_inner), lambda b, c: (b, c, 0)),
                  pl.BlockSpec((1, SSM_CHUNK, small.shape[2]), lambda b, c: (b, c, 0)),
                  pl.BlockSpec((SSM_CONV_WIDTH, conv_dim), full),
                  pl.BlockSpec((1, conv_dim), full),
                  pl.BlockSpec((1, LANES), full),
                  pl.BlockSpec((1, LANES), full),
                  pl.BlockSpec((1, d_inner), full),
                  pl.BlockSpec((1, d_inner), full),
                  pl.BlockSpec((1, SSM_CHUNK, d_model), lambda b, c: (b, c, 0)),
                  pl.BlockSpec((d_model, d_extra), full)],
        out_specs=[pl.BlockSpec((1, SSM_CHUNK, d_inner), lambda b, c: (b, c, 0)),
                   pl.BlockSpec((1, SSM_CHUNK, d_extra), lambda b, c: (b, c, 0))],
        out_shape=[jax.ShapeDtypeStruct((bsz, seq, d_inner), BF16),
                   jax.ShapeDtypeStruct((bsz, seq, d_extra), BF16)],
        scratch_shapes=[pltpu.VMEM((SSM_CHUNK + 2 * SUBLANES, conv_dim), F32),
                        pltpu.VMEM((d_inner // LANES, SSM_D_STATE, LANES), F32),
                        pltpu.VMEM((SSM_CHUNK, d_inner), F32)],
        compiler_params=_params("arbitrary", "arbitrary"),
        name="ssd_scan",
    )(xbc, zs, small, conv_w.astype(F32), conv_b.reshape(1, conv_dim).astype(F32),
      dtb, alog, d_row, norm_g.reshape(1, d_inner).astype(F32),
      u.reshape(bsz, seq, d_model), w_extra)


def _head_sum(x, ind_ref, indt_ref):
    s = _split_dot(x, ind_ref[...])
    return _split_dot(s, indt_ref[...])


def _rwkv_prep_kernel(x_ref, xh_ref, s_ref, sh_ref, mu_ref, mus_ref, w0_ref, wup_ref, wupl_ref,
                      a0_ref, aup_ref, aupl_ref, kk_ref, ka_ref, rk_ref, ind_ref, indt_ref,
                      r_o, lw_o, k_o, v_o, a_o, b_o, bonus_o, gs_o):
    i = pl.program_id(1)
    tm = x_ref.shape[1]
    d = r_o.shape[2]
    not_first = (i > 0).astype(F32)

    def shifted(cur, halo):
        prev_row = halo[SUBLANES - 1:SUBLANES, :] * not_first
        rolled = pltpu.roll(cur, 1, 0)
        rid = lax.broadcasted_iota(jnp.int32, cur.shape, 0)
        return jnp.where(rid == 0, prev_row, rolled)

    x = x_ref[0]
    rw = x + (shifted(x, xh_ref[0]) - x) * mu_ref[...]
    s = s_ref[0]
    rs = s + (shifted(s, sh_ref[0]) - s) * mus_ref[...]

    r = rw[:, 0:d]
    k = rw[:, d:2 * d]
    v = rw[:, 2 * d:3 * d]
    g = rw[:, 3 * d:4 * d]

    w_arg = w0_ref[...] + _dot_hi_lo(jnp.tanh(rs), wup_ref, wupl_ref)
    w_log = -_softplus(-w_arg) - 0.5
    lw_o[0] = -jnp.exp(w_log)
    a = jax.nn.sigmoid(a0_ref[...] + _dot_hi_lo(rs, aup_ref, aupl_ref))

    kk = k * kk_ref[...]
    kk = kk * jnp.minimum(lax.rsqrt(_head_sum(kk * kk, ind_ref, indt_ref)), 1e12)
    k2 = k * (1.0 + (a - 1.0) * ka_ref[...])
    r_o[0] = r.astype(r_o.dtype)
    k_o[0] = k2.astype(k_o.dtype)
    v_o[0] = v.astype(v_o.dtype)
    a_o[0] = (-kk).astype(a_o.dtype)
    b_o[0] = (kk * a).astype(b_o.dtype)
    bonus_o[0] = _head_sum(r * k2 * rk_ref[...], ind_ref, indt_ref) * v
    gs_o[0] = _silu(g).astype(gs_o.dtype)


def _rwkv_intra_kernel(r_ref, lw_ref, k_ref, v_ref, a_ref, b_ref,
                       rh_o, ah_o, bh_o, kh_o, yh_o, vh_o, gc_o,
                       q_ref, apl_ref, tpl_ref, aa_ref, ar_ref, at_ref, rt_ref):
    bsz, tb, uw = r_ref.shape
    c = RWKV_CHUNK
    n_c = tb // c
    nt = (((1,), (1,)), ((), ()))

    ti = lax.broadcasted_iota(jnp.int32, (c, c), 0)
    si = lax.broadcasted_iota(jnp.int32, (c, c), 1)
    tri = (si <= ti).astype(BF16)
    pw = 2 * c
    n_pairs = UNIT_HEADS // 2
    prow = lambda q: slice(q * pw, (q + 1) * pw)
    rowt = lax.broadcasted_iota(jnp.int32, (pw, pw), 0) % c
    cols = lax.broadcasted_iota(jnp.int32, (pw, pw), 1) % c
    strict = cols < rowt
    incl = cols <= rowt
    even = lax.broadcasted_iota(jnp.int32, (c, pw), 1) < c
    f32 = lambda ref, b, rows: ref[b, rows, :].astype(F32)

    grp = range(INTRA_GROUP)
    for b in range(bsz):
        def stage_a(gi_, carry, b=b):
            cis = [gi_ * INTRA_GROUP + g for g in grp]
            js = [b * n_c + ci for ci in cis]
            rows = [pl.ds(pl.multiple_of(ci * c, c), c) for ci in cis]
            lws = [lw_ref[b, rows[g], :] for g in grp]
            css = [_cumsum_rows(tri, lws[g]) for g in grp]
            lhss, rhss = [], []
            for g in grp:
                cs = css[g]
                dec = jnp.exp(cs)
                inv = jnp.exp(-cs)
                at = f32(a_ref, b, rows[g]) * jnp.exp(cs - lws[g])
                rt = f32(r_ref, b, rows[g]) * dec
                bt = f32(b_ref, b, rows[g]) * inv
                kt = f32(k_ref, b, rows[g]) * inv
                gc = dec[c - 1:c, :]
                bh_o[b, rows[g], :] = (bt * gc).astype(bh_o.dtype)
                kh_o[b, rows[g], :] = (kt * gc).astype(kh_o.dtype)
                gc_o[b, cis[g]] = jnp.broadcast_to(gc, (SUBLANES, uw))
                at_ref[js[g]] = at.astype(BF16)
                rt_ref[js[g]] = rt
                for q in range(n_pairs):
                    btq, ktq = bt[:, prow(q)], kt[:, prow(q)]
                    lhss.append(jnp.concatenate([at[:, prow(q)], rt[:, prow(q)]],
                                                axis=0).astype(BF16))
                    rhss.append(jnp.concatenate(
                        [jnp.where(even, btq, 0.0), jnp.where(even, ktq, 0.0),
                         jnp.where(even, 0.0, btq), jnp.where(even, 0.0, ktq)],
                        axis=0).astype(BF16))
            prods = [lax.dot_general(lhs, rhs, nt, preferred_element_type=F32)
                     for lhs, rhs in zip(lhss, rhss)]
            for i, prod in enumerate(prods):
                g, q = divmod(i, n_pairs)
                stack = lambda x: jnp.concatenate([x[:, 0:pw], x[:, pw:]], axis=0)
                da = jnp.where(strict, stack(prod[0:c]), 0.0)
                dr = jnp.where(incl, stack(prod[c:]), 0.0)
                aa_ref[js[g], prow(q), :] = da.astype(BF16)
                ar_ref[js[g], prow(q), :] = dr.astype(BF16)
                for e in range(2):
                    q_ref[pl.ds(UNIT_HEADS * js[g] + 2 * q + e, c, stride=Q_STRIDE), :] = (
                        jnp.where(even, da[e * c:(e + 1) * c], 0.0))
            return carry
        lax.fori_loop(0, n_c // INTRA_GROUP, stage_a, 0)

    for t in range(c):
        tile = q_ref[t * Q_STRIDE:t * Q_STRIDE + LANES, :]
        apl_ref[t] = tile.T[0:c, :]
    sub = lax.broadcasted_iota(jnp.int32, (SUBLANES, LANES), 0)
    zero8 = jnp.zeros((SUBLANES, LANES), F32)
    for t in range(c):
        nb = t // SUBLANES + 1
        acc = [[None, None] for _ in range(nb)]
        for s in range(t):
            coef = apl_ref[t, s:s + 1, :]
            for jv in range(s // SUBLANES + 1):
                term = coef * tpl_ref[s, jv * SUBLANES:(jv + 1) * SUBLANES, :]
                slot = acc[jv]
                slot[s % 2] = term if slot[s % 2] is None else slot[s % 2] + term
        for jv in range(c // SUBLANES):
            if jv < nb:
                parts = [p for p in acc[jv] if p is not None]
                if jv == nb - 1:
                    parts.append(jnp.where(sub == t % SUBLANES, 1.0, 0.0))
                val = parts[0]
                for p in parts[1:]:
                    val = val + p
            else:
                val = zero8
            tpl_ref[t, jv * SUBLANES:(jv + 1) * SUBLANES, :] = val
    zpad = jnp.zeros((LANES - c, LANES), F32)
    for t in range(c):
        q_ref[t * Q_STRIDE:t * Q_STRIDE + LANES, :] = jnp.concatenate([tpl_ref[t], zpad], axis=0).T

    for b in range(bsz):
        def stage_c(gi_, carry, b=b):
            cis = [gi_ * INTRA_GROUP + g for g in grp]
            js = [b * n_c + ci for ci in cis]
            rows = [pl.ds(pl.multiple_of(ci * c, c), c) for ci in cis]
            items = [(g, q) for g in grp for q in range(n_pairs)]
            zb = jnp.zeros((c, pw), BF16)

            def pick(res):
                return jnp.where(even, res[0:c], res[c:])

            vs = [v_ref[b, rows[g], prow(q)] for g, q in items]
            akvs = [pick(jnp.dot(aa_ref[js[g], prow(q), :],
                                 jnp.concatenate([zb, vs[i]], axis=0),
                                 preferred_element_type=F32))
                    for i, (g, q) in enumerate(items)]
            tsts = [jnp.concatenate(
                [q_ref[pl.ds(UNIT_HEADS * js[g] + 2 * q + e, c, stride=Q_STRIDE), :][:, 0:c]
                 for e in range(2)], axis=0).astype(BF16) for g, q in items]
            xs = [jnp.dot(tsts[i],
                          jnp.concatenate([at_ref[js[g], :, prow(q)], akvs[i].astype(BF16)], axis=1),
                          preferred_element_type=F32)
                  for i, (g, q) in enumerate(items)]
            ahs = [pick(x[:, 0:pw]) for x in xs]
            vhs = [pick(x[:, pw:]) for x in xs]
            ys = []
            for i, (g, q) in enumerate(items):
                top = jnp.concatenate([ahs[i].astype(BF16), vhs[i].astype(BF16)], axis=1)
                bot = jnp.concatenate([zb, vs[i]], axis=1)
                ys.append(jnp.dot(ar_ref[js[g], prow(q), :], jnp.concatenate([top, bot], axis=0),
                                  preferred_element_type=F32))
            for i, (g, q) in enumerate(items):
                rh_o[b, rows[g], prow(q)] = (rt_ref[js[g], :, prow(q)]
                                             + pick(ys[i][:, 0:pw])).astype(rh_o.dtype)
                ah_o[b, rows[g], prow(q)] = ahs[i].astype(ah_o.dtype)
                yh_o[b, rows[g], prow(q)] = pick(ys[i][:, pw:])
                vh_o[b, rows[g], prow(q)] = vhs[i]
            return carry
        lax.fori_loop(0, n_c // INTRA_GROUP, stage_c, 0)


def _rwkv_seq_kernel(rh_ref, ah_ref, bh_ref, kh_ref, v_ref, yh_ref, vh_ref, gc_ref, y_o, s_ref):
    bsz, tb, d = rh_ref.shape
    c = RWKV_CHUNK
    uw = UNIT_W
    n_units = d // uw
    nt = (((1,), (1,)), ((), ()))
    tn = (((0,), (0,)), ((), ()))

    @pl.when(pl.program_id(0) == 0)
    def _():
        s_ref[...] = jnp.zeros(s_ref.shape, F32)

    rid = lax.broadcasted_iota(jnp.int32, (uw, uw), 0) // RWKV_HEAD_DIM
    cid = lax.broadcasted_iota(jnp.int32, (uw, uw), 1) // RWKV_HEAD_DIM
    same_head = rid == cid
    units = [(b, u) for b in range(bsz) for u in range(n_units)]

    for ci in range(tb // c):
        rows = slice(ci * c, (ci + 1) * c)
        prods = []
        for i, (b, u) in enumerate(units):
            ul = slice(u * uw, (u + 1) * uw)
            lhs = jnp.concatenate([rh_ref[b, rows, ul], ah_ref[b, rows, ul]], axis=0)
            prods.append(lax.dot_general(lhs, s_ref[i].astype(BF16), nt,
                                         preferred_element_type=F32))
        us = []
        for i, (b, u) in enumerate(units):
            ul = slice(u * uw, (u + 1) * uw)
            yu = jnp.concatenate([yh_ref[b, rows, ul], vh_ref[b, rows, ul]], axis=0) + prods[i]
            y_o[b, rows, ul] = yu[0:c]
            us.append(yu[c:].astype(BF16))
        for i, (b, u) in enumerate(units):
            ul = slice(u * uw, (u + 1) * uw)
            lhs = jnp.concatenate([us[i], v_ref[b, rows, ul]], axis=0)
            rhs = jnp.concatenate([bh_ref[b, rows, ul], kh_ref[b, rows, ul]], axis=0)
            upd = lax.dot_general(lhs, rhs, tn, preferred_element_type=F32)
            s_ref[i] = s_ref[i] * gc_ref[b, ci, 0:1, ul] + jnp.where(same_head, upd, 0.0)


def _rwkv_post_kernel(y_ref, bonus_ref, gs_ref, gg_ref, gb_ref, ind_ref, indt_ref, o_ref):
    y = y_ref[...]
    inv = 1.0 / RWKV_HEAD_DIM
    mu = _head_sum(y, ind_ref, indt_ref) * inv
    dlt = y - mu
    var = _head_sum(dlt * dlt, ind_ref, indt_ref) * inv
    o = dlt * lax.rsqrt(var + RWKV_GN_EPS) * gg_ref[...] + gb_ref[...]
    o_ref[...] = ((o + bonus_ref[...]) * gs_ref[...].astype(F32)).astype(o_ref.dtype)


def _rwkv_branch(rkvg, small, mu, w0, w_up, a0, a_up, k_k, k_a, r_k, gn_g, gn_b):
    bsz, seq, d4 = rkvg.shape
    d = d4 // 4
    n_heads = d // RWKV_HEAD_DIM
    sw = small.shape[2]
    lo = sw - 2 * RWKV_LORA
    mu = mu.astype(F32)
    mu_main = mu[:4 * d].reshape(1, 4 * d)
    mu_small = jnp.pad(mu[4 * d:], (lo, 0)).reshape(1, sw)
    wup_pad = jnp.pad(w_up.astype(F32), ((lo, RWKV_LORA), (0, 0)))
    aup_pad = jnp.pad(a_up.astype(F32), ((lo + RWKV_LORA, 0), (0, 0)))
    hi_lo = lambda w: (w.astype(BF16), (w - w.astype(BF16).astype(F32)).astype(BF16))
    wup_hi, wup_lo = hi_lo(wup_pad)
    aup_hi, aup_lo = hi_lo(aup_pad)
    head_of = jnp.arange(d) // RWKV_HEAD_DIM
    ind = (head_of[:, None] == jnp.arange(LANES)[None, :]).astype(BF16)
    indt = ind.T
    vec = lambda t: t.reshape(1, d).astype(F32)

    tm = 128
    nt = seq // tm
    hb = tm // SUBLANES
    cur = lambda w: pl.BlockSpec((1, tm, w), lambda b, i: (b, i, 0))
    halo = lambda w: pl.BlockSpec((1, SUBLANES, w), lambda b, i: (b, jnp.maximum(i * hb - 1, 0), 0))
    full = lambda s0, s1: pl.BlockSpec((s0, s1), lambda b, i: (0, 0))
    outs = pl.pallas_call(
        _rwkv_prep_kernel,
        grid=(bsz, nt),
        in_specs=[cur(4 * d), halo(4 * d), cur(sw), halo(sw), full(1, 4 * d), full(1, sw),
                  full(1, d), full(sw, d), full(sw, d), full(1, d), full(sw, d), full(sw, d),
                  full(1, d), full(1, d), full(1, d), full(d, LANES), full(LANES, d)],
        out_specs=[cur(d)] * 8,
        out_shape=[jax.ShapeDtypeStruct((bsz, seq, d), dt)
                   for dt in (BF16, F32, BF16, BF16, BF16, BF16, F32, BF16)],
        compiler_params=_params("parallel", "parallel"),
        name="rwkv_prep",
    )(rkvg, rkvg, small, small, mu_main, mu_small, vec(w0), wup_hi, wup_lo, vec(a0), aup_hi,
      aup_lo, vec(k_k), vec(k_a), vec(r_k), ind, indt)
    r, lw, k, v, na, kb, bonus, gs = outs

    c = RWKV_CHUNK
    uw = UNIT_W
    n_c = LANES // (bsz * UNIT_HEADS)
    tb = n_c * c
    n_prob = bsz * n_c
    blk = pl.BlockSpec((bsz, tb, uw), lambda i, u: (0, i, u))
    shp = lambda dt: jax.ShapeDtypeStruct((bsz, seq, d), dt)
    rh, ah, bh, kh, yh, vh, gc = pl.pallas_call(
        _rwkv_intra_kernel,
        grid=(seq // tb, d // uw),
        in_specs=[blk] * 6,
        out_specs=[blk] * 6 + [pl.BlockSpec((bsz, n_c, SUBLANES, uw), lambda i, u: (0, i, 0, u))],
        out_shape=[shp(BF16), shp(BF16), shp(BF16), shp(BF16), shp(F32), shp(F32),
                   jax.ShapeDtypeStruct((bsz, seq // c, SUBLANES, d), F32)],
        scratch_shapes=[pltpu.VMEM((c * Q_STRIDE, LANES), F32),
                        pltpu.VMEM((c, c, LANES), F32),
                        pltpu.VMEM((c, c, LANES), F32),
                        pltpu.VMEM((n_prob, UNIT_HEADS * c, 2 * c), BF16),
                        pltpu.VMEM((n_prob, UNIT_HEADS * c, 2 * c), BF16),
                        pltpu.VMEM((n_prob, c, uw), BF16),
                        pltpu.VMEM((n_prob, c, uw), F32)],
        compiler_params=_params("parallel", "parallel"),
        name="rwkv_intra",
    )(r, lw, k, v, na, kb)

    ts = 2 * c
    sblk = pl.BlockSpec((bsz, ts, d), lambda i: (0, i, 0))
    y = pl.pallas_call(
        _rwkv_seq_kernel,
        grid=(seq // ts,),
        in_specs=[sblk] * 7 + [pl.BlockSpec((bsz, ts // c, SUBLANES, d), lambda i: (0, i, 0, 0))],
        out_specs=sblk,
        out_shape=shp(F32),
        scratch_shapes=[pltpu.VMEM((bsz * d // uw, uw, uw), F32)],
        compiler_params=_params("arbitrary"),
        name="rwkv_seq",
    )(rh, ah, bh, kh, v, yh, vh, gc)

    m = bsz * seq
    tp = 256
    rows = lambda: pl.BlockSpec((tp, d), lambda i: (i, 0))
    c1 = lambda s0, s1: pl.BlockSpec((s0, s1), lambda i: (0, 0))
    return pl.pallas_call(
        _rwkv_post_kernel,
        grid=(m // tp,),
        in_specs=[rows(), rows(), rows(), c1(1, d), c1(1, d), c1(d, LANES), c1(LANES, d)],
        out_specs=rows(),
        out_shape=jax.ShapeDtypeStruct((m, d), BF16),
        compiler_params=_params("parallel"),
        name="rwkv_post",
    )(y.reshape(m, d), bonus.reshape(m, d), gs.reshape(m, d), vec(gn_g), vec(gn_b), ind, indt)


def _mem_attn_kernel(q_ref, k_ref, v_ref, g_ref, o_ref):
    d = q_ref.shape[2]
    hd = d // MEM_HEADS
    scale = hd ** -0.5
    for h in range(MEM_HEADS):
        sl = slice(h * hd, (h + 1) * hd)
        s = lax.dot_general(q_ref[0, :, sl], k_ref[0, :, sl], (((1,), (1,)), ((), ())),
                            preferred_element_type=F32) * scale
        s = s - jnp.max(s, axis=-1, keepdims=True)
        e = jnp.exp(s)
        p = e / jnp.sum(e, axis=-1, keepdims=True)
        om = jnp.dot(p.astype(BF16), v_ref[0, :, sl], preferred_element_type=F32)
        o_ref[0, :, sl] = (om * _silu(g_ref[0, :, sl].astype(F32))).astype(o_ref.dtype)


def _mem_branch(qg, mem_k, mem_v):
    bsz, seq, d2 = qg.shape
    d = d2 // 2
    n_mem = mem_k.shape[1]
    tm = 512
    cur = pl.BlockSpec((1, tm, d), lambda b, i: (b, i, 0))
    gate = pl.BlockSpec((1, tm, d), lambda b, i: (b, i, 1))
    kv = pl.BlockSpec((1, n_mem, d), lambda b, i: (b, 0, 0))
    return pl.pallas_call(
        _mem_attn_kernel,
        grid=(bsz, seq // tm),
        in_specs=[cur, kv, kv, gate],
        out_specs=cur,
        out_shape=jax.ShapeDtypeStruct((bsz, seq, d), BF16),
        compiler_params=_params("parallel", "parallel"),
        name="mem_attn",
    )(qg, mem_k, mem_v, qg)


def _merge_kernel(a1_ref, a2_ref, a3_ref, w1_ref, w2_ref, w3_ref, g1_ref, g2_ref, g3_ref,
                  b1_ref, b2_ref, b3_ref, o_ref):
    def term(a_ref, w_ref, g_ref, b_ref):
        y = jnp.dot(a_ref[...], w_ref[...], preferred_element_type=F32)
        return jax.nn.sigmoid(g_ref[...] + b_ref[...]) * y

    acc = term(a1_ref, w1_ref, g1_ref, b1_ref)
    acc = acc + term(a2_ref, w2_ref, g2_ref, b2_ref)
    acc = acc + term(a3_ref, w3_ref, g3_ref, b3_ref)
    o_ref[...] = acc.astype(o_ref.dtype)


def _merge(a_ssm, a_rwkv, a_mem, w_ssm, w_rwkv, w_mem, gate_raw, gate_b):
    m, d = a_rwkv.shape
    tm, tn = 512, 512
    nb = d // tn
    act = lambda a: pl.BlockSpec((tm, a.shape[1]), lambda j, i: (i, 0))
    wgt = lambda w: pl.BlockSpec((w.shape[0], tn), lambda j, i: (0, j))
    gate = lambda k: pl.BlockSpec((tm, tn), lambda j, i: (i, j + k * nb))
    bias = lambda k: pl.BlockSpec((1, tn), lambda j, i: (0, j + k * nb))
    gb = gate_b.reshape(1, 3 * d).astype(F32)
    return pl.pallas_call(
        _merge_kernel,
        grid=(nb, m // tm),
        in_specs=[act(a_ssm), act(a_rwkv), act(a_mem), wgt(w_ssm), wgt(w_rwkv), wgt(w_mem),
                  gate(0), gate(1), gate(2), bias(0), bias(1), bias(2)],
        out_specs=pl.BlockSpec((tm, tn), lambda j, i: (i, j)),
        out_shape=jax.ShapeDtypeStruct((m, d), BF16),
        compiler_params=_params("parallel", "parallel"),
        name="merge",
    )(a_ssm, a_rwkv, a_mem, w_ssm, w_rwkv, w_mem, gate_raw, gate_raw, gate_raw, gb, gb, gb)


def _out_kernel(m_ref, w_ref, g_ref, x_ref, o_ref):
    y = jnp.dot(m_ref[...], w_ref[...], preferred_element_type=F32)
    ms = jnp.mean(y * y, axis=-1, keepdims=True)
    o_ref[...] = x_ref[...] + y * lax.rsqrt(ms + RMS_EPS) * g_ref[...]


def _out_proj(merged, w_out, g_post, x2d):
    m, d = x2d.shape
    tm = 512
    rows = pl.BlockSpec((tm, d), lambda i: (i, 0))
    return pl.pallas_call(
        _out_kernel,
        grid=(m // tm,),
        in_specs=[rows, pl.BlockSpec((d, d), lambda i: (0, 0)),
                  pl.BlockSpec((1, d), lambda i: (0, 0)), rows],
        out_specs=rows,
        out_shape=jax.ShapeDtypeStruct((m, d), F32),
        compiler_params=_params("parallel"),
        name="out_proj",
    )(merged, w_out, g_post.reshape(1, d).astype(F32), x2d)


def _layer(h, mem, g_pre, w_in, ssm_conv_w, ssm_conv_b, ssm_dt_bias, ssm_a_log, ssm_d, ssm_norm_g,
           rwkv_mu, rwkv_w0, rwkv_w_up, rwkv_a0, rwkv_a_up, rwkv_k_k, rwkv_k_a, rwkv_r_k,
           rwkv_gn_g, rwkv_gn_b, mem_norm_g, mem_w_kv, gate_b, w_br_ssm, w_br_rwkv, w_br_mem,
           w_out, g_post):
    bsz, seq, d = h.shape
    m = bsz * seq
    d_inner = w_br_ssm.shape[0]
    n_ssm_heads = d_inner // SSM_HEAD_DIM
    conv_dim = ssm_conv_w.shape[1]
    d_rwkv = w_br_rwkv.shape[0]
    d_mem = w_br_mem.shape[0]
    x2d = h.reshape(m, d)

    o_z = 0
    o_xbc = o_z + d_inner
    o_dt = o_xbc + conv_dim
    o_rw = o_dt + n_ssm_heads
    o_lora = o_rw + 4 * d_rwkv
    o_q = o_lora + 2 * RWKV_LORA
    o_mg = o_q + d_mem
    o_gate = o_mg + d_mem
    wb = lambda a, b: w_in[:, a:b].astype(BF16)
    w_small = jnp.concatenate([w_in[:, o_dt:o_rw], w_in[:, o_lora:o_q]], axis=1).astype(BF16)

    u = _rmsnorm(x2d, g_pre, BF16, 512)
    zs = _matmul_silu(u, wb(o_z, o_xbc), BF16, 1024, 1024, "proj_z")
    xbc = _matmul(u, wb(o_xbc, o_dt), F32, 2048, 1024, "proj_xbc")
    rkvg = _matmul(u, wb(o_rw, o_lora), F32, 2048, 1024, "proj_rwkv")
    small = _matmul(u, w_small, F32, 1024, w_small.shape[1], "proj_small")
    gate_raw = _matmul(u, wb(o_gate, w_in.shape[1]), F32, 2048, 1024, "proj_gate")

    a_ssm, mem_qg = _ssd_branch(xbc.reshape(bsz, seq, conv_dim), zs.reshape(bsz, seq, d_inner),
                                small.reshape(bsz, seq, -1), ssm_conv_w, ssm_conv_b, ssm_dt_bias,
                                ssm_a_log, ssm_d, ssm_norm_g, u, wb(o_q, o_gate))
    a_ssm = a_ssm.reshape(m, d_inner)

    a_rwkv = _rwkv_branch(rkvg.reshape(bsz, seq, 4 * d_rwkv), small.reshape(bsz, seq, -1),
                          rwkv_mu, rwkv_w0, rwkv_w_up, rwkv_a0, rwkv_a_up, rwkv_k_k, rwkv_k_a,
                          rwkv_r_k, rwkv_gn_g, rwkv_gn_b)

    n_mem = mem.shape[1]
    mem_n = _rmsnorm(mem.reshape(bsz * n_mem, d), mem_norm_g, BF16, n_mem)
    mem_kv = _matmul(mem_n, mem_w_kv.astype(BF16), BF16, bsz * n_mem, 1024, "proj_memkv")
    mem_k = mem_kv[:, :d_mem].reshape(bsz, n_mem, d_mem)
    mem_v = mem_kv[:, d_mem:].reshape(bsz, n_mem, d_mem)
    a_mem = _mem_branch(mem_qg, mem_k, mem_v).reshape(m, d_mem)

    merged = _merge(a_ssm, a_rwkv, a_mem, w_br_ssm.astype(BF16), w_br_rwkv.astype(BF16),
                    w_br_mem.astype(BF16), gate_raw, gate_b)
    out = _out_proj(merged, w_out.astype(BF16), g_post, x2d)
    return out.reshape(bsz, seq, d)


def kernel(x, mem, g_pre, w_in, ssm_conv_w, ssm_conv_b, ssm_dt_bias, ssm_a_log, ssm_d, ssm_norm_g, rwkv_mu, rwkv_w0, rwkv_w_up, rwkv_a0, rwkv_a_up, rwkv_k_k, rwkv_k_a, rwkv_r_k, rwkv_gn_g, rwkv_gn_b, mem_norm_g, mem_w_kv, gate_b, w_br_ssm, w_br_rwkv, w_br_mem, w_out, g_post):
    layer_params = (g_pre, w_in, ssm_conv_w, ssm_conv_b, ssm_dt_bias, ssm_a_log, ssm_d, ssm_norm_g,
                    rwkv_mu, rwkv_w0, rwkv_w_up, rwkv_a0, rwkv_a_up, rwkv_k_k, rwkv_k_a, rwkv_r_k,
                    rwkv_gn_g, rwkv_gn_b, mem_norm_g, mem_w_kv, gate_b, w_br_ssm, w_br_rwkv,
                    w_br_mem, w_out, g_post)
    h = x
    for layer in range(g_pre.shape[0]):
        h = _layer(h, mem, *[p[layer] for p in layer_params])
    return h
```

```python
import functools

import jax
import jax.numpy as jnp
from jax import lax
from jax.experimental import pallas as pl
from jax.experimental.pallas import tpu as pltpu

F32 = jnp.float32
BF16 = jnp.bfloat16

RMS_EPS = 1e-6
LOG2E = 1.4426950408889634
SSM_HEAD_DIM = 64
SSM_N_GROUPS = 8
SSM_D_STATE = 128
SSM_CONV_WIDTH = 4
SSM_CHUNK = 128
RWKV_HEAD_DIM = 64
RWKV_LORA = 96
RWKV_GN_EPS = 64e-5
RWKV_CHUNK = 64
UNIT_HEADS = 4
UNIT_W = UNIT_HEADS * RWKV_HEAD_DIM
PREP_SLAB = 512
INTRA_GROUP = 8
MEM_HEADS = 4

LANES = 128
SUBLANES = 8
MXU_N = 256
VMEM_LIMIT = 56 * 1024 * 1024
Q_STRIDE = LANES + SUBLANES


def _params(*sem):
    return pltpu.CompilerParams(dimension_semantics=sem, vmem_limit_bytes=VMEM_LIMIT)


def _silu(x):
    return x * jax.nn.sigmoid(x)


def _softplus(x):
    return jnp.maximum(x, 0.0) + jnp.log(1.0 + jnp.exp(-jnp.abs(x)))


def _split_dot(x, w_bf16):
    hi = x.astype(BF16)
    lo = (x - hi.astype(F32)).astype(BF16)
    return (jnp.dot(hi, w_bf16, preferred_element_type=F32)
            + jnp.dot(lo, w_bf16, preferred_element_type=F32))


def _cumsum_rows(tri_bf16, x):
    hi = x.astype(BF16)
    r1 = x - hi.astype(F32)
    mid = r1.astype(BF16)
    lo = (r1 - mid.astype(F32)).astype(BF16)
    dot = lambda p: jnp.dot(tri_bf16, p, preferred_element_type=F32)
    return dot(hi) + (dot(mid) + dot(lo))


def _dot_hi_lo(x, w_hi_ref, w_lo_ref):
    hi = x.astype(BF16)
    lo = (x - hi.astype(F32)).astype(BF16)
    w_hi = w_hi_ref[...]
    return (jnp.dot(hi, w_hi, preferred_element_type=F32)
            + (jnp.dot(hi, w_lo_ref[...], preferred_element_type=F32)
               + jnp.dot(lo, w_hi, preferred_element_type=F32)))


def _rmsnorm_kernel(x_ref, g_ref, o_ref):
    x = x_ref[...].astype(F32)
    ms = jnp.mean(x * x, axis=-1, keepdims=True)
    o_ref[...] = (x * lax.rsqrt(ms + RMS_EPS) * g_ref[...]).astype(o_ref.dtype)


def _rmsnorm(x2d, g, out_dtype, tm):
    m, d = x2d.shape
    return pl.pallas_call(
        _rmsnorm_kernel,
        grid=(m // tm,),
        in_specs=[pl.BlockSpec((tm, d), lambda i: (i, 0)),
                  pl.BlockSpec((1, d), lambda i: (0, 0))],
        out_specs=pl.BlockSpec((tm, d), lambda i: (i, 0)),
        out_shape=jax.ShapeDtypeStruct((m, d), out_dtype),
        compiler_params=_params("parallel"),
        name="rmsnorm",
    )(x2d, g.reshape(1, d).astype(F32))


def _mm_kernel(a_ref, w_ref, o_ref):
    o_ref[...] = jnp.dot(a_ref[...], w_ref[...],
                         preferred_element_type=F32).astype(o_ref.dtype)


def _matmul(a, w, out_dtype, tm, tn, name):
    m, k = a.shape
    n = w.shape[1]
    tm = min(tm, m)
    tn = min(tn, n)
    return pl.pallas_call(
        _mm_kernel,
        grid=(n // tn, m // tm),
        in_specs=[pl.BlockSpec((tm, k), lambda j, i: (i, 0)),
                  pl.BlockSpec((k, tn), lambda j, i: (0, j))],
        out_specs=pl.BlockSpec((tm, tn), lambda j, i: (i, j)),
        out_shape=jax.ShapeDtypeStruct((m, n), out_dtype),
        compiler_params=_params("parallel", "parallel"),
        name=name,
    )(a, w)


def _mm_silu_kernel(a_ref, w_ref, o_ref):
    a = a_ref[...]
    for n in range(o_ref.shape[1] // MXU_N):
        sl = slice(n * MXU_N, (n + 1) * MXU_N)
        acc = jnp.dot(a, w_ref[:, sl], preferred_element_type=F32)
        o_ref[:, sl] = _silu(acc).astype(o_ref.dtype)


def _matmul_silu(a, w, out_dtype, tm, tn, name):
    m, k = a.shape
    n = w.shape[1]
    return pl.pallas_call(
        _mm_silu_kernel,
        grid=(n // tn, m // tm),
        in_specs=[pl.BlockSpec((tm, k), lambda j, i: (i, 0)),
                  pl.BlockSpec((k, tn), lambda j, i: (0, j))],
        out_specs=pl.BlockSpec((tm, tn), lambda j, i: (i, j)),
        out_shape=jax.ShapeDtypeStruct((m, n), out_dtype),
        compiler_params=_params("parallel", "parallel"),
        name=name,
    )(a, w)


def _ssd_kernel(xbc_ref, zs_ref, sm_ref, cw_ref, cb_ref, dtb_ref, alog_ref, d_ref, ng_ref,
                u_ref, wx_ref, o_ref, px_ref, xpad_ref, state_ref, y_ref):
    c = pl.program_id(1)
    d_inner = zs_ref.shape[2]
    d_bc = SSM_N_GROUPS * SSM_D_STATE
    n_pairs = d_inner // LANES
    pairs_per_group = n_pairs // SSM_N_GROUPS
    L = SSM_CHUNK

    @pl.when(c == 0)
    def _():
        xpad_ref[0:SUBLANES, :] = jnp.zeros((SUBLANES, xpad_ref.shape[1]), F32)
        state_ref[...] = jnp.zeros(state_ref.shape, F32)

    xpad_ref[SUBLANES:SUBLANES + L, :] = xbc_ref[0]
    acc = cb_ref[...] + cw_ref[3:4, :] * xpad_ref[SUBLANES:SUBLANES + L, :]
    for kk in range(SSM_CONV_WIDTH - 1):
        shift = SSM_CONV_WIDTH - 1 - kk
        acc = acc + cw_ref[kk:kk + 1, :] * xpad_ref[SUBLANES - shift:SUBLANES - shift + L, :]
    xpad_ref[0:SUBLANES, :] = xpad_ref[L:L + SUBLANES, :]
    act = _silu(acc)

    row = lax.broadcasted_iota(jnp.int32, (L, L), 0)
    col = lax.broadcasted_iota(jnp.int32, (L, L), 1)
    causal = col <= row
    tri = causal.astype(BF16)
    first_head = col < SSM_HEAD_DIM

    dt = _softplus(sm_ref[0][:, :LANES] + dtb_ref[...])
    a_neg = -jnp.exp(alog_ref[...]) * LOG2E
    acs = _cumsum_rows(tri, dt * a_neg)
    acs_t = acs.T
    dt_t = dt.T

    px_w = px_ref.shape[2] // SSM_N_GROUPS
    u_rows = u_ref[0]

    for g in range(SSM_N_GROUPS):
        px_ref[0, :, g * px_w:(g + 1) * px_w] = jnp.dot(
            u_rows, wx_ref[:, g * px_w:(g + 1) * px_w],
            preferred_element_type=F32).astype(px_ref.dtype)
        bg = act[:, d_inner + g * SSM_D_STATE:d_inner + (g + 1) * SSM_D_STATE]
        cg = act[:, d_inner + d_bc + g * SSM_D_STATE:d_inner + d_bc + (g + 1) * SSM_D_STATE]
        bg_b = bg.astype(BF16)
        cg_b = cg.astype(BF16)
        cb = lax.dot_general(cg_b, bg_b, (((1,), (1,)), ((), ())),
                             preferred_element_type=F32)
        bgt_b = bg.T.astype(BF16)
        for q in range(pairs_per_group):
            p = g * pairs_per_group + q
            xp = act[:, p * LANES:(p + 1) * LANES]
            xp_b = xp.astype(BF16)
            ys = []
            cols = []
            dcols = []
            for e in range(2):
                h = 2 * p + e
                col_b = jnp.broadcast_to(acs[:, h:h + 1], (L, L))
                row_b = jnp.broadcast_to(acs_t[h:h + 1, :], (L, L))
                dtrow_b = jnp.broadcast_to(dt_t[h:h + 1, :], (L, L))
                decay = jnp.where(causal, jnp.exp2(col_b - row_b), 0.0)
                mix = (cb * decay * dtrow_b).astype(BF16)
                ys.append(jnp.dot(mix, xp_b, preferred_element_type=F32))
                cols.append(col_b)
                dcols.append(jnp.broadcast_to(dt[:, h:h + 1], (L, L)))
            y_intra = jnp.where(first_head, ys[0], ys[1])
            colsel = jnp.where(first_head, cols[0], cols[1])
            dtsel = jnp.where(first_head, dcols[0], dcols[1])
            st = state_ref[p]
            y_inter = jnp.dot(cg_b, st.astype(BF16), preferred_element_type=F32) * jnp.exp2(colsel)
            y_ref[:, p * LANES:(p + 1) * LANES] = (
                y_intra + y_inter + d_ref[:, p * LANES:(p + 1) * LANES] * xp)
            last = colsel[L - 1:L, :]
            xw = (xp * (jnp.exp2(last - colsel) * dtsel)).astype(BF16)
            state_ref[p] = st * jnp.exp2(last) + jnp.dot(bgt_b, xw, preferred_element_type=F32)

    gw = d_inner // SSM_N_GROUPS
    for g in range(SSM_N_GROUPS):
        sl = slice(g * gw, (g + 1) * gw)
        t = y_ref[:, sl] * zs_ref[0, :, sl].astype(F32)
        ms = jnp.mean(t * t, axis=-1, keepdims=True)
        o_ref[0, :, sl] = (t * lax.rsqrt(ms + RMS_EPS) * ng_ref[:, sl]).astype(o_ref.dtype)


def _ssd_branch(xbc, zs, small, conv_w, conv_b, dt_bias, a_log, d_skip, norm_g, u, w_extra):
    bsz, seq, conv_dim = xbc.shape
    d_model, d_extra = w_extra.shape
    d_inner = zs.shape[2]
    n_heads = d_inner // SSM_HEAD_DIM
    nc = seq // SSM_CHUNK
    pad = LANES - n_heads
    dtb = jnp.pad(dt_bias.astype(F32), (0, pad)).reshape(1, LANES)
    alog = jnp.pad(a_log.astype(F32), (0, pad)).reshape(1, LANES)
    d_row = jnp.repeat(d_skip.astype(F32), SSM_HEAD_DIM).reshape(1, d_inner)
    full = lambda b, c: (0, 0)
    return pl.pallas_call(
        _ssd_kernel,
        grid=(bsz, nc),
        in_specs=[pl.BlockSpec((1, SSM_CHUNK, conv_dim), lambda b, c: (b, c, 0)),
                  pl.BlockSpec((1, SSM_CHUNK, d_inner), lambda b, c: (b, c, 0)),
                  pl.BlockSpec((1, SSM_CHUNK, small.shape[2]), lambda b, c: (b, c, 0)),
                  pl.BlockSpec((SSM_CONV_WIDTH, conv_dim), full),
                  pl.BlockSpec((1, conv_dim), full),
                  pl.BlockSpec((1, LANES), full),
                  pl.BlockSpec((1, LANES), full),
                  pl.BlockSpec((1, d_inner), full),
                  pl.BlockSpec((1, d_inner), full),
                  pl.BlockSpec((1, SSM_CHUNK, d_model), lambda b, c: (b, c, 0)),
                  pl.BlockSpec((d_model, d_extra), full)],
        out_specs=[pl.BlockSpec((1, SSM_CHUNK, d_inner), lambda b, c: (b, c, 0)),
                   pl.BlockSpec((1, SSM_CHUNK, d_extra), lambda b, c: (b, c, 0))],
        out_shape=[jax.ShapeDtypeStruct((bsz, seq, d_inner), BF16),
                   jax.ShapeDtypeStruct((bsz, seq, d_extra), F32)],
        scratch_shapes=[pltpu.VMEM((SSM_CHUNK + 2 * SUBLANES, conv_dim), F32),
                        pltpu.VMEM((d_inner // LANES, SSM_D_STATE, LANES), F32),
                        pltpu.VMEM((SSM_CHUNK, d_inner), F32)],
        compiler_params=_params("arbitrary", "arbitrary"),
        name="ssd_scan",
    )(xbc, zs, small, conv_w.astype(F32), conv_b.reshape(1, conv_dim).astype(F32),
      dtb, alog, d_row, norm_g.reshape(1, d_inner).astype(F32),
      u.reshape(bsz, seq, d_model), w_extra)


def _head_sum(x, ind_ref, indt_ref):
    s = _split_dot(x, ind_ref[...])
    return _split_dot(s, indt_ref[...])


def _rwkv_prep_kernel(x_ref, xh_ref, s_ref, sh_ref, mu_ref, mus_ref, w0_ref, wup_ref, wupl_ref,
                      a0_ref, aup_ref, aupl_ref, kk_ref, ka_ref, rk_ref, ind_ref, indt_ref,
                      r_o, lw_o, k_o, v_o, a_o, b_o, bonus_o, gs_o, a_scr):
    i = pl.program_id(1)
    tm = x_ref.shape[1]
    d = r_o.shape[2]
    not_first = (i > 0).astype(F32)

    def shifted(cur, halo):
        prev_row = halo[SUBLANES - 1:SUBLANES, :] * not_first
        rolled = pltpu.roll(cur, 1, 0)
        rid = lax.broadcasted_iota(jnp.int32, cur.shape, 0)
        return jnp.where(rid == 0, prev_row, rolled)

    s = s_ref[0]
    rs = s + (shifted(s, sh_ref[0]) - s) * mus_ref[...]

    w_arg = w0_ref[...] + _dot_hi_lo(jnp.tanh(rs), wup_ref, wupl_ref)
    w_log = -_softplus(-w_arg) - 0.5
    lw_o[0] = -jnp.exp(w_log)
    a_scr[...] = jax.nn.sigmoid(a0_ref[...] + _dot_hi_lo(rs, aup_ref, aupl_ref))

    for j in range(d // PREP_SLAB):
        sl = slice(j * PREP_SLAB, (j + 1) * PREP_SLAB)

        def mixed(part, sl=sl):
            cs = slice(part * d + sl.start, part * d + sl.stop)
            cur = x_ref[0, :, cs]
            return cur + (shifted(cur, xh_ref[0, :, cs]) - cur) * mu_ref[:, cs]

        def head_sum(val, sl=sl):
            tot = _split_dot(val, ind_ref[sl, :])
            return _split_dot(tot, indt_ref[:, sl])

        r, k, v, g = mixed(0), mixed(1), mixed(2), mixed(3)
        a = a_scr[:, sl]
        kk = k * kk_ref[:, sl]
        kk = kk * jnp.minimum(lax.rsqrt(head_sum(kk * kk)), 1e12)
        k2 = k * (1.0 + (a - 1.0) * ka_ref[:, sl])
        r_o[0, :, sl] = r.astype(r_o.dtype)
        k_o[0, :, sl] = k2.astype(k_o.dtype)
        v_o[0, :, sl] = v.astype(v_o.dtype)
        a_o[0, :, sl] = (-kk).astype(a_o.dtype)
        b_o[0, :, sl] = (kk * a).astype(b_o.dtype)
        bonus_o[0, :, sl] = head_sum(r * k2 * rk_ref[:, sl]) * v
        gs_o[0, :, sl] = _silu(g).astype(gs_o.dtype)


def _rwkv_intra_kernel(r_ref, lw_ref, k_ref, v_ref, a_ref, b_ref,
                       rh_o, ah_o, bh_o, kh_o, yh_o, vh_o, gc_o,
                       q_ref, apl_ref, tpl_ref, aa_ref, ar_ref, at_ref, rt_ref):
    bsz, tb, uw = r_ref.shape
    c = RWKV_CHUNK
    n_c = tb // c
    nt = (((1,), (1,)), ((), ()))

    ti = lax.broadcasted_iota(jnp.int32, (c, c), 0)
    si = lax.broadcasted_iota(jnp.int32, (c, c), 1)
    tri = (si <= ti).astype(BF16)
    pw = 2 * c
    n_pairs = UNIT_HEADS // 2
    prow = lambda q: slice(q * pw, (q + 1) * pw)
    rowt = lax.broadcasted_iota(jnp.int32, (pw, pw), 0) % c
    cols = lax.broadcasted_iota(jnp.int32, (pw, pw), 1) % c
    strict = cols < rowt
    incl = cols <= rowt
    even = lax.broadcasted_iota(jnp.int32, (c, pw), 1) < c
    f32 = lambda ref, b, rows: ref[b, rows, :].astype(F32)

    grp = range(INTRA_GROUP)
    for b in range(bsz):
        def stage_a(gi_, carry, b=b):
            cis = [gi_ * INTRA_GROUP + g for g in grp]
            js = [b * n_c + ci for ci in cis]
            rows = [pl.ds(pl.multiple_of(ci * c, c), c) for ci in cis]
            lws = [lw_ref[b, rows[g], :] for g in grp]
            css = [_cumsum_rows(tri, lws[g]) for g in grp]
            lhss, rhss = [], []
            for g in grp:
                cs = css[g]
                dec = jnp.exp(cs)
                inv = jnp.exp(-cs)
                at = f32(a_ref, b, rows[g]) * jnp.exp(cs - lws[g])
                rt = f32(r_ref, b, rows[g]) * dec
                bt = f32(b_ref, b, rows[g]) * inv
                kt = f32(k_ref, b, rows[g]) * inv
                gc = dec[c - 1:c, :]
                bh_o[b, rows[g], :] = (bt * gc).astype(bh_o.dtype)
                kh_o[b, rows[g], :] = (kt * gc).astype(kh_o.dtype)
                gc_o[b, cis[g]] = jnp.broadcast_to(gc, (SUBLANES, uw))
                at_ref[js[g]] = at.astype(BF16)
                rt_ref[js[g]] = rt
                for q in range(n_pairs):
                    btq, ktq = bt[:, prow(q)], kt[:, prow(q)]
                    lhss.append(jnp.concatenate([at[:, prow(q)], rt[:, prow(q)]],
                                                axis=0).astype(BF16))
                    rhss.append(jnp.concatenate(
                        [jnp.where(even, btq, 0.0), jnp.where(even, ktq, 0.0),
                         jnp.where(even, 0.0, btq), jnp.where(even, 0.0, ktq)],
                        axis=0).astype(BF16))
            prods = [lax.dot_general(lhs, rhs, nt, preferred_element_type=F32)
                     for lhs, rhs in zip(lhss, rhss)]
            for i, prod in enumerate(prods):
                g, q = divmod(i, n_pairs)
                stack = lambda x: jnp.concatenate([x[:, 0:pw], x[:, pw:]], axis=0)
                da = jnp.where(strict, stack(prod[0:c]), 0.0)
                dr = jnp.where(incl, stack(prod[c:]), 0.0)
                aa_ref[js[g], prow(q), :] = da.astype(BF16)
                ar_ref[js[g], prow(q), :] = dr.astype(BF16)
                for e in range(2):
                    q_ref[pl.ds(UNIT_HEADS * js[g] + 2 * q + e, c, stride=Q_STRIDE), :] = (
                        jnp.where(even, da[e * c:(e + 1) * c], 0.0))
            return carry
        lax.fori_loop(0, n_c // INTRA_GROUP, stage_a, 0)

    for t in range(c):
        tile = q_ref[t * Q_STRIDE:t * Q_STRIDE + LANES, :]
        apl_ref[t] = tile.T[0:c, :]
    sub = lax.broadcasted_iota(jnp.int32, (SUBLANES, LANES), 0)
    zero8 = jnp.zeros((SUBLANES, LANES), F32)
    for t in range(c):
        nb = t // SUBLANES + 1
        acc = [[None, None] for _ in range(nb)]
        for s in range(t):
            coef = apl_ref[t, s:s + 1, :]
            for jv in range(s // SUBLANES + 1):
                term = coef * tpl_ref[s, jv * SUBLANES:(jv + 1) * SUBLANES, :]
                slot = acc[jv]
                slot[s % 2] = term if slot[s % 2] is None else slot[s % 2] + term
        for jv in range(c // SUBLANES):
            if jv < nb:
                parts = [p for p in acc[jv] if p is not None]
                if jv == nb - 1:
                    parts.append(jnp.where(sub == t % SUBLANES, 1.0, 0.0))
                val = parts[0]
                for p in parts[1:]:
                    val = val + p
            else:
                val = zero8
            tpl_ref[t, jv * SUBLANES:(jv + 1) * SUBLANES, :] = val
    zpad = jnp.zeros((LANES - c, LANES), F32)
    for t in range(c):
        q_ref[t * Q_STRIDE:t * Q_STRIDE + LANES, :] = jnp.concatenate([tpl_ref[t], zpad], axis=0).T

    for b in range(bsz):
        def stage_c(gi_, carry, b=b):
            cis = [gi_ * INTRA_GROUP + g for g in grp]
            js = [b * n_c + ci for ci in cis]
            rows = [pl.ds(pl.multiple_of(ci * c, c), c) for ci in cis]
            items = [(g, q) for g in grp for q in range(n_pairs)]
            zb = jnp.zeros((c, pw), BF16)

            def pick(res):
                return jnp.where(even, res[0:c], res[c:])

            vs = [v_ref[b, rows[g], prow(q)] for g, q in items]
            akvs = [pick(jnp.dot(aa_ref[js[g], prow(q), :],
                                 jnp.concatenate([zb, vs[i]], axis=0),
                                 preferred_element_type=F32))
                    for i, (g, q) in enumerate(items)]
            tsts = [jnp.concatenate(
                [q_ref[pl.ds(UNIT_HEADS * js[g] + 2 * q + e, c, stride=Q_STRIDE), :][:, 0:c]
                 for e in range(2)], axis=0).astype(BF16) for g, q in items]
            xs = [jnp.dot(tsts[i],
                          jnp.concatenate([at_ref[js[g], :, prow(q)], akvs[i].astype(BF16)], axis=1),
                          preferred_element_type=F32)
                  for i, (g, q) in enumerate(items)]
            ahs = [pick(x[:, 0:pw]) for x in xs]
            vhs = [pick(x[:, pw:]) for x in xs]
            ys = []
            for i, (g, q) in enumerate(items):
                top = jnp.concatenate([ahs[i].astype(BF16), vhs[i].astype(BF16)], axis=1)
                bot = jnp.concatenate([zb, vs[i]], axis=1)
                ys.append(jnp.dot(ar_ref[js[g], prow(q), :], jnp.concatenate([top, bot], axis=0),
                                  preferred_element_type=F32))
            for i, (g, q) in enumerate(items):
                rh_o[b, rows[g], prow(q)] = (rt_ref[js[g], :, prow(q)]
                                             + pick(ys[i][:, 0:pw])).astype(rh_o.dtype)
                ah_o[b, rows[g], prow(q)] = ahs[i].astype(ah_o.dtype)
                yh_o[b, rows[g], prow(q)] = pick(ys[i][:, pw:])
                vh_o[b, rows[g], prow(q)] = vhs[i]
            return carry
        lax.fori_loop(0, n_c // INTRA_GROUP, stage_c, 0)


def _rwkv_seq_kernel(rh_ref, ah_ref, bh_ref, kh_ref, v_ref, yh_ref, vh_ref, gc_ref, y_o, s_ref):
    bsz, tb, d = rh_ref.shape
    c = RWKV_CHUNK
    uw = UNIT_W
    n_units = d // uw
    nt = (((1,), (1,)), ((), ()))
    tn = (((0,), (0,)), ((), ()))

    @pl.when(pl.program_id(0) == 0)
    def _():
        s_ref[...] = jnp.zeros(s_ref.shape, F32)

    rid = lax.broadcasted_iota(jnp.int32, (uw, uw), 0) // RWKV_HEAD_DIM
    cid = lax.broadcasted_iota(jnp.int32, (uw, uw), 1) // RWKV_HEAD_DIM
    same_head = rid == cid
    units = [(b, u) for b in range(bsz) for u in range(n_units)]

    for ci in range(tb // c):
        rows = slice(ci * c, (ci + 1) * c)
        prods = []
        for i, (b, u) in enumerate(units):
            ul = slice(u * uw, (u + 1) * uw)
            lhs = jnp.concatenate([rh_ref[b, rows, ul], ah_ref[b, rows, ul]], axis=0)
            prods.append(lax.dot_general(lhs, s_ref[i].astype(BF16), nt,
                                         preferred_element_type=F32))
        us = []
        for i, (b, u) in enumerate(units):
            ul = slice(u * uw, (u + 1) * uw)
            yu = jnp.concatenate([yh_ref[b, rows, ul], vh_ref[b, rows, ul]], axis=0) + prods[i]
            y_o[b, rows, ul] = yu[0:c]
            us.append(yu[c:].astype(BF16))
        for i, (b, u) in enumerate(units):
            ul = slice(u * uw, (u + 1) * uw)
            lhs = jnp.concatenate([us[i], v_ref[b, rows, ul]], axis=0)
            rhs = jnp.concatenate([bh_ref[b, rows, ul], kh_ref[b, rows, ul]], axis=0)
            upd = lax.dot_general(lhs, rhs, tn, preferred_element_type=F32)
            s_ref[i] = s_ref[i] * gc_ref[b, ci, 0:1, ul] + jnp.where(same_head, upd, 0.0)


def _rwkv_post_kernel(y_ref, bonus_ref, gs_ref, gg_ref, gb_ref, ind_ref, indt_ref, o_ref):
    y = y_ref[...]
    inv = 1.0 / RWKV_HEAD_DIM
    mu = _head_sum(y, ind_ref, indt_ref) * inv
    dlt = y - mu
    var = _head_sum(dlt * dlt, ind_ref, indt_ref) * inv
    o = dlt * lax.rsqrt(var + RWKV_GN_EPS) * gg_ref[...] + gb_ref[...]
    o_ref[...] = ((o + bonus_ref[...]) * gs_ref[...].astype(F32)).astype(o_ref.dtype)


def _rwkv_branch(rkvg, small, mu, w0, w_up, a0, a_up, k_k, k_a, r_k, gn_g, gn_b):
    bsz, seq, d4 = rkvg.shape
    d = d4 // 4
    n_heads = d // RWKV_HEAD_DIM
    sw = small.shape[2]
    lo = sw - 2 * RWKV_LORA
    mu = mu.astype(F32)
    mu_main = mu[:4 * d].reshape(1, 4 * d)
    mu_small = jnp.pad(mu[4 * d:], (lo, 0)).reshape(1, sw)
    wup_pad = jnp.pad(w_up.astype(F32), ((lo, RWKV_LORA), (0, 0)))
    aup_pad = jnp.pad(a_up.astype(F32), ((lo + RWKV_LORA, 0), (0, 0)))
    hi_lo = lambda w: (w.astype(BF16), (w - w.astype(BF16).astype(F32)).astype(BF16))
    wup_hi, wup_lo = hi_lo(wup_pad)
    aup_hi, aup_lo = hi_lo(aup_pad)
    head_of = jnp.arange(d) // RWKV_HEAD_DIM
    ind = (head_of[:, None] == jnp.arange(LANES)[None, :]).astype(BF16)
    indt = ind.T
    vec = lambda t: t.reshape(1, d).astype(F32)

    tm = 128
    nt = seq // tm
    hb = tm // SUBLANES
    cur = lambda w: pl.BlockSpec((1, tm, w), lambda b, i: (b, i, 0))
    halo = lambda w: pl.BlockSpec((1, SUBLANES, w), lambda b, i: (b, jnp.maximum(i * hb - 1, 0), 0))
    full = lambda s0, s1: pl.BlockSpec((s0, s1), lambda b, i: (0, 0))
    outs = pl.pallas_call(
        _rwkv_prep_kernel,
        grid=(bsz, nt),
        in_specs=[cur(4 * d), halo(4 * d), cur(sw), halo(sw), full(1, 4 * d), full(1, sw),
                  full(1, d), full(sw, d), full(sw, d), full(1, d), full(sw, d), full(sw, d),
                  full(1, d), full(1, d), full(1, d), full(d, LANES), full(LANES, d)],
        out_specs=[cur(d)] * 8,
        out_shape=[jax.ShapeDtypeStruct((bsz, seq, d), dt)
                   for dt in (BF16, F32, BF16, BF16, BF16, BF16, F32, BF16)],
        scratch_shapes=[pltpu.VMEM((tm, d), F32)],
        compiler_params=_params("parallel", "parallel"),
        name="rwkv_prep",
    )(rkvg, rkvg, small, small, mu_main, mu_small, vec(w0), wup_hi, wup_lo, vec(a0), aup_hi,
      aup_lo, vec(k_k), vec(k_a), vec(r_k), ind, indt)
    r, lw, k, v, na, kb, bonus, gs = outs

    c = RWKV_CHUNK
    uw = UNIT_W
    n_c = LANES // (bsz * UNIT_HEADS)
    tb = n_c * c
    n_prob = bsz * n_c
    blk = pl.BlockSpec((bsz, tb, uw), lambda i, u: (0, i, u))
    shp = lambda dt: jax.ShapeDtypeStruct((bsz, seq, d), dt)
    rh, ah, bh, kh, yh, vh, gc = pl.pallas_call(
        _rwkv_intra_kernel,
        grid=(seq // tb, d // uw),
        in_specs=[blk] * 6,
        out_specs=[blk] * 6 + [pl.BlockSpec((bsz, n_c, SUBLANES, uw), lambda i, u: (0, i, 0, u))],
        out_shape=[shp(BF16), shp(BF16), shp(BF16), shp(BF16), shp(F32), shp(F32),
                   jax.ShapeDtypeStruct((bsz, seq // c, SUBLANES, d), F32)],
        scratch_shapes=[pltpu.VMEM((c * Q_STRIDE, LANES), F32),
                        pltpu.VMEM((c, c, LANES), F32),
                        pltpu.VMEM((c, c, LANES), F32),
                        pltpu.VMEM((n_prob, UNIT_HEADS * c, 2 * c), BF16),
                        pltpu.VMEM((n_prob, UNIT_HEADS * c, 2 * c), BF16),
                        pltpu.VMEM((n_prob, c, uw), BF16),
                        pltpu.VMEM((n_prob, c, uw), F32)],
        compiler_params=_params("parallel", "parallel"),
        name="rwkv_intra",
    )(r, lw, k, v, na, kb)

    ts = 2 * c
    sblk = pl.BlockSpec((bsz, ts, d), lambda i: (0, i, 0))
    y = pl.pallas_call(
        _rwkv_seq_kernel,
        grid=(seq // ts,),
        in_specs=[sblk] * 7 + [pl.BlockSpec((bsz, ts // c, SUBLANES, d), lambda i: (0, i, 0, 0))],
        out_specs=sblk,
        out_shape=shp(F32),
        scratch_shapes=[pltpu.VMEM((bsz * d // uw, uw, uw), F32)],
        compiler_params=_params("arbitrary"),
        name="rwkv_seq",
    )(rh, ah, bh, kh, v, yh, vh, gc)

    m = bsz * seq
    tp = 256
    rows = lambda: pl.BlockSpec((tp, d), lambda i: (i, 0))
    c1 = lambda s0, s1: pl.BlockSpec((s0, s1), lambda i: (0, 0))
    return pl.pallas_call(
        _rwkv_post_kernel,
        grid=(m // tp,),
        in_specs=[rows(), rows(), rows(), c1(1, d), c1(1, d), c1(d, LANES), c1(LANES, d)],
        out_specs=rows(),
        out_shape=jax.ShapeDtypeStruct((m, d), BF16),
        compiler_params=_params("parallel"),
        name="rwkv_post",
    )(y.reshape(m, d), bonus.reshape(m, d), gs.reshape(m, d), vec(gn_g), vec(gn_b), ind, indt)


def _mem_attn_kernel(q_ref, k_ref, v_ref, g_ref, o_ref):
    d = q_ref.shape[2]
    hd = d // MEM_HEADS
    scale = hd ** -0.5
    for h in range(MEM_HEADS):
        sl = slice(h * hd, (h + 1) * hd)
        s = lax.dot_general(q_ref[0, :, sl], k_ref[0, :, sl], (((1,), (1,)), ((), ())),
                            preferred_element_type=F32) * scale
        s = s - jnp.max(s, axis=-1, keepdims=True)
        e = jnp.exp(s)
        p = e / jnp.sum(e, axis=-1, keepdims=True)
        om = jnp.dot(p.astype(BF16), v_ref[0, :, sl], preferred_element_type=F32)
        o_ref[0, :, sl] = (om * _silu(g_ref[0, :, sl])).astype(o_ref.dtype)


def _mem_branch(q, gate, mem_k, mem_v):
    bsz, seq, d = q.shape
    n_mem = mem_k.shape[1]
    tm = 512
    cur = pl.BlockSpec((1, tm, d), lambda b, i: (b, i, 0))
    kv = pl.BlockSpec((1, n_mem, d), lambda b, i: (b, 0, 0))
    return pl.pallas_call(
        _mem_attn_kernel,
        grid=(bsz, seq // tm),
        in_specs=[cur, kv, kv, cur],
        out_specs=cur,
        out_shape=jax.ShapeDtypeStruct((bsz, seq, d), BF16),
        compiler_params=_params("parallel", "parallel"),
        name="mem_attn",
    )(q, mem_k, mem_v, gate)


def _merge_kernel(a1_ref, a2_ref, a3_ref, w1_ref, w2_ref, w3_ref, g1_ref, g2_ref, g3_ref,
                  b1_ref, b2_ref, b3_ref, o_ref):
    def term(a_ref, w_ref, g_ref, b_ref):
        y = jnp.dot(a_ref[...], w_ref[...], preferred_element_type=F32)
        return jax.nn.sigmoid(g_ref[...] + b_ref[...]) * y

    acc = term(a1_ref, w1_ref, g1_ref, b1_ref)
    acc = acc + term(a2_ref, w2_ref, g2_ref, b2_ref)
    acc = acc + term(a3_ref, w3_ref, g3_ref, b3_ref)
    o_ref[...] = acc.astype(o_ref.dtype)


def _merge(a_ssm, a_rwkv, a_mem, w_ssm, w_rwkv, w_mem, gate_raw, gate_b):
    m, d = a_rwkv.shape
    tm, tn = 512, 512
    nb = d // tn
    act = lambda a: pl.BlockSpec((tm, a.shape[1]), lambda j, i: (i, 0))
    wgt = lambda w: pl.BlockSpec((w.shape[0], tn), lambda j, i: (0, j))
    gate = lambda k: pl.BlockSpec((tm, tn), lambda j, i: (i, j + k * nb))
    bias = lambda k: pl.BlockSpec((1, tn), lambda j, i: (0, j + k * nb))
    gb = gate_b.reshape(1, 3 * d).astype(F32)
    return pl.pallas_call(
        _merge_kernel,
        grid=(nb, m // tm),
        in_specs=[act(a_ssm), act(a_rwkv), act(a_mem), wgt(w_ssm), wgt(w_rwkv), wgt(w_mem),
                  gate(0), gate(1), gate(2), bias(0), bias(1), bias(2)],
        out_specs=pl.BlockSpec((tm, tn), lambda j, i: (i, j)),
        out_shape=jax.ShapeDtypeStruct((m, d), BF16),
        compiler_params=_params("parallel", "parallel"),
        name="merge",
    )(a_ssm, a_rwkv, a_mem, w_ssm, w_rwkv, w_mem, gate_raw, gate_raw, gate_raw, gb, gb, gb)


def _out_kernel(m_ref, w_ref, g_ref, x_ref, o_ref):
    y = jnp.dot(m_ref[...], w_ref[...], preferred_element_type=F32)
    ms = jnp.mean(y * y, axis=-1, keepdims=True)
    o_ref[...] = x_ref[...] + y * lax.rsqrt(ms + RMS_EPS) * g_ref[...]


def _out_proj(merged, w_out, g_post, x2d):
    m, d = x2d.shape
    tm = 512
    rows = pl.BlockSpec((tm, d), lambda i: (i, 0))
    return pl.pallas_call(
        _out_kernel,
        grid=(m // tm,),
        in_specs=[rows, pl.BlockSpec((d, d), lambda i: (0, 0)),
                  pl.BlockSpec((1, d), lambda i: (0, 0)), rows],
        out_specs=rows,
        out_shape=jax.ShapeDtypeStruct((m, d), F32),
        compiler_params=_params("parallel"),
        name="out_proj",
    )(merged, w_out, g_post.reshape(1, d).astype(F32), x2d)


def _layer(h, mem, g_pre, w_in, ssm_conv_w, ssm_conv_b, ssm_dt_bias, ssm_a_log, ssm_d, ssm_norm_g,
           rwkv_mu, rwkv_w0, rwkv_w_up, rwkv_a0, rwkv_a_up, rwkv_k_k, rwkv_k_a, rwkv_r_k,
           rwkv_gn_g, rwkv_gn_b, mem_norm_g, mem_w_kv, gate_b, w_br_ssm, w_br_rwkv, w_br_mem,
           w_out, g_post):
    bsz, seq, d = h.shape
    m = bsz * seq
    d_inner = w_br_ssm.shape[0]
    n_ssm_heads = d_inner // SSM_HEAD_DIM
    conv_dim = ssm_conv_w.shape[1]
    d_rwkv = w_br_rwkv.shape[0]
    d_mem = w_br_mem.shape[0]
    x2d = h.reshape(m, d)

    o_z = 0
    o_xbc = o_z + d_inner
    o_dt = o_xbc + conv_dim
    o_rw = o_dt + n_ssm_heads
    o_lora = o_rw + 4 * d_rwkv
    o_q = o_lora + 2 * RWKV_LORA
    o_mg = o_q + d_mem
    o_gate = o_mg + d_mem
    wb = lambda a, b: w_in[:, a:b].astype(BF16)
    w_small = jnp.concatenate([w_in[:, o_dt:o_rw], w_in[:, o_lora:o_q]], axis=1).astype(BF16)

    u = _rmsnorm(x2d, g_pre, BF16, 512)
    zs = _matmul_silu(u, wb(o_z, o_xbc), BF16, 1024, 1024, "proj_z")
    xbc = _matmul(u, wb(o_xbc, o_dt), F32, 2048, 1024, "proj_xbc")
    rkvg = _matmul(u, wb(o_rw, o_lora), F32, 2048, 1024, "proj_rwkv")
    small = _matmul(u, w_small, F32, 1024, w_small.shape[1], "proj_small")
    mem_q = _matmul(u, wb(o_q, o_mg), BF16, 2048, 1024, "proj_memq")
    gate_raw = _matmul(u, wb(o_gate, w_in.shape[1]), F32, 2048, 1024, "proj_gate")

    a_ssm, mem_g = _ssd_branch(xbc.reshape(bsz, seq, conv_dim), zs.reshape(bsz, seq, d_inner),
                               small.reshape(bsz, seq, -1), ssm_conv_w, ssm_conv_b, ssm_dt_bias,
                               ssm_a_log, ssm_d, ssm_norm_g, u, wb(o_mg, o_gate))
    a_ssm = a_ssm.reshape(m, d_inner)

    a_rwkv = _rwkv_branch(rkvg.reshape(bsz, seq, 4 * d_rwkv), small.reshape(bsz, seq, -1),
                          rwkv_mu, rwkv_w0, rwkv_w_up, rwkv_a0, rwkv_a_up, rwkv_k_k, rwkv_k_a,
                          rwkv_r_k, rwkv_gn_g, rwkv_gn_b)

    n_mem = mem.shape[1]
    mem_n = _rmsnorm(mem.reshape(bsz * n_mem, d), mem_norm_g, BF16, n_mem)
    mem_kv = _matmul(mem_n, mem_w_kv.astype(BF16), BF16, bsz * n_mem, 1024, "proj_memkv")
    mem_k = mem_kv[:, :d_mem].reshape(bsz, n_mem, d_mem)
    mem_v = mem_kv[:, d_mem:].reshape(bsz, n_mem, d_mem)
    a_mem = _mem_branch(mem_q.reshape(bsz, seq, d_mem), mem_g.reshape(bsz, seq, d_mem),
                        mem_k, mem_v).reshape(m, d_mem)

    merged = _merge(a_ssm, a_rwkv, a_mem, w_br_ssm.astype(BF16), w_br_rwkv.astype(BF16),
                    w_br_mem.astype(BF16), gate_raw, gate_b)
    out = _out_proj(merged, w_out.astype(BF16), g_post, x2d)
    return out.reshape(bsz, seq, d)


def kernel(x, mem, g_pre, w_in, ssm_conv_w, ssm_conv_b, ssm_dt_bias, ssm_a_log, ssm_d, ssm_norm_g, rwkv_mu, rwkv_w0, rwkv_w_up, rwkv_a0, rwkv_a_up, rwkv_k_k, rwkv_k_a, rwkv_r_k, rwkv_gn_g, rwkv_gn_b, mem_norm_g, mem_w_kv, gate_b, w_br_ssm, w_br_rwkv, w_br_mem, w_out, g_post):
    layer_params = (g_pre, w_in, ssm_conv_w, ssm_conv_b, ssm_dt_bias, ssm_a_log, ssm_d, ssm_norm_g,
                    rwkv_mu, rwkv_w0, rwkv_w_up, rwkv_a0, rwkv_a_up, rwkv_k_k, rwkv_k_a, rwkv_r_k,
                    rwkv_gn_g, rwkv_gn_b, mem_norm_g, mem_w_kv, gate_b, w_br_ssm, w_br_rwkv,
                    w_br_mem, w_out, g_post)
    h = x
    for layer in range(g_pre.shape[0]):
        h = _layer(h, mem, *[p[layer] for p in layer_params])
    return h
```
